```python
import jax, jax.numpy as jnp
from jax import lax
import numpy as np

D_MODEL = 2048
BATCH = 2
SEQ = 4096
DEPTH = 4

N_MIXERS = 2
EPS = 1e-6

HGRN_EXPAND = 128
HGRN_HEADS = D_MODEL // HGRN_EXPAND
HGRN_DK = HGRN_EXPAND
HGRN_DV = D_MODEL // HGRN_HEADS
HGRN_CHUNK = 64
HGRN_IN = 5 * D_MODEL

HEAD_DIM = 128
ATTN_HEADS = D_MODEL // HEAD_DIM
DILATED_GROUPS = ((128, 1), (512, 4), (2048, 16))
N_GROUPS = len(DILATED_GROUPS)
ATTN_IN = N_GROUPS * 3 * ATTN_HEADS * HEAD_DIM
ROPE_THETA = 500000.0
ROPE_DIM = HEAD_DIM // 4
NEG_INF = -1e30

D_FF = ((8 * D_MODEL // 3 + 255) // 256) * 256

N_HGRN_LAYERS = (DEPTH + 1) // 2
N_ATTN_LAYERS = DEPTH // 2

kernel_name = "hybrid_hgrn2_dilated_attn_encoder"


def rmsnorm(x, w):
    x32 = x.astype(jnp.float32)
    y = x32 * lax.rsqrt(jnp.mean(x32 * x32, axis=-1, keepdims=True) + EPS)
    return (y * w.astype(jnp.float32)).astype(x.dtype)


def chunk_gla(q, k, v, log_f):
    B, H, S, dk = q.shape
    dv = v.shape[-1]
    n = S // HGRN_CHUNK

    def to_chunks(t):
        return jnp.moveaxis(t.reshape(B, H, n, HGRN_CHUNK, t.shape[-1]), 2, 0)

    tri = jnp.tril(jnp.ones((HGRN_CHUNK, HGRN_CHUNK), dtype=bool))

    def step(state, inp):
        qc, kc, vc, gc = inp
        b = jnp.cumsum(gc, axis=2)
        diff = b[:, :, :, None, :] - b[:, :, None, :, :]
        decay = jnp.exp(jnp.where(tri[:, :, None], diff, -jnp.inf))
        scores = jnp.einsum('bhtd,bhsd,bhtsd->bhts', qc, kc, decay)
        o = (jnp.einsum('bhts,bhsv->bhtv', scores, vc)
             + jnp.einsum('bhtd,bhdv->bhtv', qc * jnp.exp(b), state))
        b_last = b[:, :, -1:, :]
        state = (jnp.exp(b_last[:, :, 0, :, None]) * state
                 + jnp.einsum('bhsd,bhsv->bhdv', kc * jnp.exp(b_last - b), vc))
        return state, o

    state0 = jnp.zeros((B, H, dk, dv), jnp.float32)
    _, o = lax.scan(step, state0, (to_chunks(q), to_chunks(k), to_chunks(v), to_chunks(log_f)))
    return jnp.moveaxis(o, 0, 2).reshape(B, H, S, dv)


def hgrn2_mixer(h, w_in, lb_fwd, lb_bwd, g_norm_w):
    B, S, _ = h.shape
    proj = jnp.einsum('bsd,de->bse', h, w_in)
    q, f_fwd, f_bwd, i, g = jnp.split(proj, 5, axis=-1)

    def heads(t, dh):
        return t.reshape(B, S, HGRN_HEADS, dh).transpose(0, 2, 1, 3).astype(jnp.float32)

    q = heads(jax.nn.silu(q), HGRN_DK)
    v = heads(i, HGRN_DV)

    def gates(f, lb):
        fg = lb + (1.0 - lb) * jax.nn.sigmoid(f.astype(jnp.float32))
        return heads(1.0 - fg, HGRN_DK), heads(jnp.log(fg), HGRN_DK)

    k_f, lg_f = gates(f_fwd, lb_fwd)
    k_b, lg_b = gates(f_bwd, lb_bwd)
    o_f = chunk_gla(q, k_f, v, lg_f)
    o_b = jnp.flip(chunk_gla(jnp.flip(q, 2), jnp.flip(k_b, 2), jnp.flip(v, 2), jnp.flip(lg_b, 2)), 2)
    o = rmsnorm(o_f + o_b, g_norm_w)
    o = o.transpose(0, 2, 1, 3).reshape(B, S, D_MODEL).astype(h.dtype)
    return o * jax.nn.silu(g)


def partial_rope(x, cos, sin):
    xr, xp = x[..., :ROPE_DIM], x[..., ROPE_DIM:]
    x1, x2 = jnp.split(xr, 2, axis=-1)
    rot = jnp.concatenate([-x2, x1], axis=-1)
    xr = (xr * cos + rot * sin).astype(x.dtype)
    return jnp.concatenate([xr, xp], axis=-1)


def dilated_group_attention(q, k, v, window, dilation):
    B, S, H, dh = q.shape
    radius = window // (2 * dilation)
    W = radius
    L = S // dilation
    nblk = -(-L // W)
    Lp = nblk * W

    def strided(t, pad_lo, pad_hi):
        t = t.reshape(B, L, dilation, H, dh)
        return jnp.pad(t, ((0, 0), (pad_lo, pad_hi), (0, 0), (0, 0), (0, 0)))

    qs = strided(q, 0, Lp - L).reshape(B, nblk, W, dilation, H, dh).astype(jnp.float32)

    def neighbourhood(t):
        t = strided(t, W, Lp - L + W).reshape(B, nblk + 2, W, dilation, H, dh)
        return jnp.concatenate([t[:, :-2], t[:, 1:-1], t[:, 2:]], axis=2).astype(jnp.float32)

    kn = neighbourhood(k)
    vn = neighbourhood(v)
    u_q = jnp.arange(nblk)[:, None] * W + jnp.arange(W)[None, :]
    u_k = (jnp.arange(nblk)[:, None] - 1) * W + jnp.arange(3 * W)[None, :]
    rel = u_k[:, None, :] - u_q[:, :, None]
    mask = (jnp.abs(rel) <= radius) & (u_k[:, None, :] >= 0) & (u_k[:, None, :] < L)

    s = jnp.einsum('bnidhe,bnjdhe->bndhij', qs, kn) * (HEAD_DIM ** -0.5)
    s = jnp.where(mask[None, :, None, None], s, NEG_INF)
    m = jnp.max(s, axis=-1, keepdims=True)
    p = jnp.exp(s - m)
    l = jnp.sum(p, axis=-1)
    o = jnp.einsum('bndhij,bnjdhe->bnidhe', p, vn)
    l_t = l.transpose(0, 1, 4, 2, 3)
    o = o / l_t[..., None]
    lse = (m[..., 0] + jnp.log(l)).transpose(0, 1, 4, 2, 3)
    o = o.reshape(B, Lp, dilation, H, dh)[:, :L].reshape(B, S, H, dh)
    lse = lse.reshape(B, Lp, dilation, H)[:, :L].reshape(B, S, H)
    return o, lse


def dilated_mixer(h, w_in, cos, sin):
    B, S, _ = h.shape
    proj = jnp.einsum('bsd,de->bse', h, w_in).reshape(B, S, N_GROUPS, 3, ATTN_HEADS, HEAD_DIM)
    outs, lses = [], []
    for g, (window, dilation) in enumerate(DILATED_GROUPS):
        q = partial_rope(proj[:, :, g, 0], cos, sin)
        k = partial_rope(proj[:, :, g, 1], cos, sin)
        v = proj[:, :, g, 2]
        o, lse = dilated_group_attention(q, k, v, window, dilation)
        outs.append(o)
        lses.append(lse)
    wts = jax.nn.softmax(jnp.stack(lses, axis=0), axis=0)
    o = jnp.sum(wts[..., None] * jnp.stack(outs, axis=0), axis=0)
    return o.reshape(B, S, ATTN_HEADS * HEAD_DIM).astype(h.dtype)


def swiglu_ffn(h, w_in, w_out):
    gate, up = jnp.split(jnp.einsum('bsd,df->bsf', h, w_in), 2, axis=-1)
    return jnp.einsum('bsf,fd->bsd', jax.nn.silu(gate) * up, w_out)


def setup_inputs(seed: int = 0) -> dict:
    key = jax.random.key(seed)
    ks = jax.random.split(key, 9)
    f32 = jnp.float32
    x = jax.random.normal(ks[0], (BATCH, SEQ, D_MODEL), f32)
    norm_w = 1.0 + 0.02 * jax.random.normal(ks[1], (DEPTH, 4, D_MODEL), f32)
    w_in_hgrn = jax.random.normal(ks[2], (N_HGRN_LAYERS, D_MODEL, HGRN_IN), f32) * D_MODEL ** -0.5
    hgrn_lower_bounds = 0.1 * jax.random.normal(ks[3], (2, DEPTH, HGRN_HEADS * HGRN_DK), f32)
    hgrn_gnorm = 1.0 + 0.02 * jax.random.normal(ks[4], (N_HGRN_LAYERS, HGRN_DV), f32)
    w_in_attn = jax.random.normal(ks[5], (N_ATTN_LAYERS, D_MODEL, ATTN_IN), f32) * D_MODEL ** -0.5
    w_out = jax.random.normal(ks[6], (DEPTH, D_MODEL, D_MODEL), f32) * D_MODEL ** -0.5
    w_ffn_in = jax.random.normal(ks[7], (DEPTH, D_MODEL, 2 * D_FF), f32) * D_MODEL ** -0.5
    w_ffn_out = jax.random.normal(ks[8], (DEPTH, D_FF, D_MODEL), f32) * D_FF ** -0.5
    return {"x": x, "norm_w": norm_w, "w_in_hgrn": w_in_hgrn,
            "hgrn_lower_bounds": hgrn_lower_bounds, "hgrn_gnorm": hgrn_gnorm,
            "w_in_attn": w_in_attn, "w_out": w_out,
            "w_ffn_in": w_ffn_in, "w_ffn_out": w_ffn_out}


def reference(x, norm_w, w_in_hgrn, hgrn_lower_bounds, hgrn_gnorm, w_in_attn, w_out, w_ffn_in, w_ffn_out):
    S = x.shape[1]
    lb = jnp.cumsum(jax.nn.softmax(hgrn_lower_bounds.astype(jnp.float32), axis=1), axis=1)
    lb = lb - lb[:, :1]
    pos = jnp.arange(S, dtype=jnp.float32)
    inv_freq = ROPE_THETA ** (-(jnp.arange(0, ROPE_DIM, 2, dtype=jnp.float32) / ROPE_DIM))
    ang = pos[:, None] * inv_freq[None, :]
    ang = jnp.concatenate([ang, ang], axis=-1)
    cos = jnp.cos(ang)[:, None, :]
    sin = jnp.sin(ang)[:, None, :]

    for layer in range(DEPTH):
        mixer = layer % N_MIXERS
        slot = layer // N_MIXERS
        h = rmsnorm(x, norm_w[layer, 0])
        if mixer == 0:
            h = hgrn2_mixer(h, w_in_hgrn[slot], lb[0, layer], lb[1, layer], hgrn_gnorm[slot])
        else:
            h = dilated_mixer(h, w_in_attn[slot], cos, sin)
        h = jnp.einsum('bsd,de->bse', h, w_out[layer])
        x = x + rmsnorm(h, norm_w[layer, 1])
        h = swiglu_ffn(rmsnorm(x, norm_w[layer, 2]), w_ffn_in[layer], w_ffn_out[layer])
        x = x + rmsnorm(h, norm_w[layer, 3])
    return x
```

```python
import functools
import math

import numpy as np
import jax
import jax.numpy as jnp
from jax import lax
from jax.experimental import pallas as pl
from jax.experimental.pallas import tpu as pltpu

F32 = jnp.float32
BF16 = jnp.bfloat16

EPS = 1e-6
HEAD = 128
DILATED_GROUPS = ((128, 1), (512, 4), (2048, 16))
ROPE_THETA = 500000.0
ROPE_DIM = HEAD // 4
NEG_INF = -1e30

GLA_CHUNK = 128
ATTN_BQ = 128
ATTN_RADIUS = 64
ATTN_KW = ATTN_BQ + 2 * ATTN_RADIUS
ATTN_P1 = 256

VMEM_LIMIT = 56 * 1024 * 1024

NT_DIMS = (((1,), (1,)), ((), ()))
TN_DIMS = (((0,), (0,)), ((), ()))


def _cparams(*sem):
    return pltpu.CompilerParams(dimension_semantics=sem, vmem_limit_bytes=VMEM_LIMIT)


def _rms(x, w):
    return x * lax.rsqrt(jnp.mean(x * x, axis=-1, keepdims=True) + EPS) * w


def _sigmoid(x):
    return 1.0 / (1.0 + jnp.exp(-x))


def _norm_kernel(*refs, has_h, has_pre):
    refs = list(refs)
    x_ref = refs.pop(0)
    x = x_ref[...]
    if has_h:
        h_ref = refs.pop(0)
        wpost_ref = refs.pop(0)
    if has_pre:
        wpre_ref = refs.pop(0)
    if has_h:
        xo_ref = refs.pop(0)
        x = x + _rms(h_ref[...].astype(F32), wpost_ref[...])
        xo_ref[...] = x
    if has_pre:
        hn_ref = refs.pop(0)
        hn_ref[...] = _rms(x, wpre_ref[...]).astype(BF16)


def _norm_call(x, h=None, w_post=None, w_pre=None, rows=256):
    T, D = x.shape
    has_h, has_pre = h is not None, w_pre is not None
    row_spec = pl.BlockSpec((rows, D), lambda i: (i, 0))
    w_spec = pl.BlockSpec((1, D), lambda i: (0, 0))
    args, in_specs, out_shape, out_specs = [x], [row_spec], [], []
    if has_h:
        args += [h, w_post.reshape(1, D)]
        in_specs += [row_spec, w_spec]
        out_shape.append(jax.ShapeDtypeStruct((T, D), F32))
        out_specs.append(row_spec)
    if has_pre:
        args.append(w_pre.reshape(1, D))
        in_specs.append(w_spec)
        out_shape.append(jax.ShapeDtypeStruct((T, D), BF16))
        out_specs.append(row_spec)
    outs = pl.pallas_call(
        functools.partial(_norm_kernel, has_h=has_h, has_pre=has_pre),
        grid=(T // rows,),
        in_specs=in_specs,
        out_specs=out_specs,
        out_shape=out_shape,
        compiler_params=_cparams("parallel"),
        name="norm_merge",
    )(*args)
    return outs if len(outs) > 1 else outs[0]


def _mm_kernel(x_ref, w_ref, o_ref, wb_ref):
    @pl.when(pl.program_id(1) == 0)
    def _():
        wb_ref[...] = w_ref[...].astype(BF16)

    o_ref[...] = jnp.dot(x_ref[...], wb_ref[...], preferred_element_type=F32).astype(o_ref.dtype)


def _matmul(x, w, layer, *, tm, tn, out_dtype):
    T, K = x.shape
    N = w.shape[2]
    return pl.pallas_call(
        _mm_kernel,
        grid=(N // tn, T // tm),
        in_specs=[
            pl.BlockSpec((tm, K), lambda j, i: (i, 0)),
            pl.BlockSpec((None, K, tn), lambda j, i: (layer, 0, j)),
        ],
        out_specs=pl.BlockSpec((tm, tn), lambda j, i: (i, j)),
        out_shape=jax.ShapeDtypeStruct((T, N), out_dtype),
        scratch_shapes=[pltpu.VMEM((K, tn), BF16)],
        compiler_params=_cparams("parallel", "arbitrary"),
        name="matmul",
    )(x, w)


def _swiglu_kernel(x_ref, wg_ref, wu_ref, o_ref, wgb_ref, wub_ref):
    @pl.when(pl.program_id(1) == 0)
    def _():
        wgb_ref[...] = wg_ref[...].astype(BF16)
        wub_ref[...] = wu_ref[...].astype(BF16)

    x = x_ref[...]
    gate = jnp.dot(x, wgb_ref[...], preferred_element_type=F32)
    up = jnp.dot(x, wub_ref[...], preferred_element_type=F32)
    o_ref[...] = (gate * _sigmoid(gate) * up).astype(o_ref.dtype)


def _matmul_swiglu(x, w, layer, *, tm, tn):
    T, K = x.shape
    F = w.shape[2] // 2
    nf = F // tn
    return pl.pallas_call(
        _swiglu_kernel,
        grid=(nf, T // tm),
        in_specs=[
            pl.BlockSpec((tm, K), lambda j, i: (i, 0)),
            pl.BlockSpec((None, K, tn), lambda j, i: (layer, 0, j)),
            pl.BlockSpec((None, K, tn), lambda j, i: (layer, 0, j + nf)),
        ],
        out_specs=pl.BlockSpec((tm, tn), lambda j, i: (i, j)),
        out_shape=jax.ShapeDtypeStruct((T, F), BF16),
        scratch_shapes=[pltpu.VMEM((K, tn), BF16), pltpu.VMEM((K, tn), BF16)],
        compiler_params=_cparams("parallel", "arbitrary"),
        name="matmul_swiglu",
    )(x, w, w)


def _lb_kernel(x_ref, o_ref, *, depth):
    for d in range(2):
        r = [x_ref[pl.ds(d * depth + i, 1), :] for i in range(depth)]
        m = functools.reduce(jnp.maximum, r)
        e = [jnp.exp(v - m) for v in r]
        tot = functools.reduce(lambda a, b: a + b, e)
        c = None
        first = None
        for i in range(depth):
            p = e[i] / tot
            c = p if c is None else c + p
            if first is None:
                first = c
            o_ref[pl.ds(d * depth + i, 1), :] = c - first


def _lb_call(lower_bounds):
    two, depth, D = lower_bounds.shape
    return pl.pallas_call(
        functools.partial(_lb_kernel, depth=depth),
        out_shape=jax.ShapeDtypeStruct((two * depth, D), F32),
        name="hgrn_lower_bounds",
    )(lower_bounds.reshape(two * depth, D).astype(F32))


def _gla_level_ids(C, rev):
    t = np.arange(C)[:, None]
    s = np.arange(C)[None, :]
    x = t ^ s
    lv = np.where(x > 0, np.floor(np.log2(np.maximum(x, 1))).astype(np.int64), int(math.log2(C)))
    allowed = (t <= s) if rev else (t >= s)
    return np.where(allowed, lv, -1).astype(np.int32)


def _gla_tri(C, rev):
    t = np.arange(C)[:, None]
    j = np.arange(C)[None, :]
    return ((j >= t) if rev else (j <= t)).astype(np.float32)


def _mid_rows(b, w, rev, tmod):
    C = b.shape[0]
    span = 2 * w
    mid = w if rev else w - 1
    if span >= 8:
        pieces = []
        for a in range(C // span):
            row = b[a * span + mid:a * span + mid + 1, :]
            pieces.append(jnp.broadcast_to(row, (span, b.shape[1])))
        return pieces[0] if len(pieces) == 1 else jnp.concatenate(pieces, axis=0)
    g = b
    for o in range(span):
        delta = mid - o
        if delta != 0:
            g = jnp.where(tmod[w] == o, pltpu.roll(b, (-delta) % C, 0), g)
    return g


def _gla_chunk(q, f_raw, v, lb, st_ref, lv_ref, tri_ref, rev):
    C = q.shape[0]
    nlev = int(math.log2(C))
    fg = lb + (1.0 - lb) * _sigmoid(f_raw)
    k = 1.0 - fg
    lg = jnp.log(fg)
    hi = lg.astype(BF16)
    r1 = lg - hi.astype(F32)
    mid = r1.astype(BF16)
    lo = (r1 - mid.astype(F32)).astype(BF16)
    tri = tri_ref[...]
    b = (jnp.dot(tri, hi, preferred_element_type=F32)
         + jnp.dot(tri, mid, preferred_element_type=F32)
         + jnp.dot(tri, lo, preferred_element_type=F32))

    tio = lax.broadcasted_iota(jnp.int32, b.shape, 0)
    tmod = {1: tio & 1, 2: tio & 3}
    lv = lv_ref[...]
    qb = q.astype(BF16)
    kb = k.astype(BF16)
    vb = v.astype(BF16)
    scores = jnp.where(lv == nlev, lax.dot_general(qb, kb, NT_DIMS, preferred_element_type=F32), 0.0)
    for p in range(nlev):
        w = 1 << p
        z = jnp.exp(-jnp.abs(b - _mid_rows(b, w, rev, tmod)))
        s = lax.dot_general((q * z).astype(BF16), (k * z).astype(BF16), NT_DIMS,
                            preferred_element_type=F32)
        scores = jnp.where(lv == p, s, scores)

    b_edge = b[0:1, :] if rev else b[C - 1:C, :]
    q_st = (q * jnp.exp(b)).astype(BF16)
    k_st = (k * jnp.exp(b_edge - b)).astype(BF16)
    st = st_ref[...]
    o = (jnp.dot(scores.astype(BF16), vb, preferred_element_type=F32)
         + lax.dot_general(q_st, st.astype(BF16), NT_DIMS, preferred_element_type=F32))
    st_ref[...] = st * jnp.exp(b_edge) + lax.dot_general(vb, k_st, TN_DIMS, preferred_element_type=F32)
    return o


def _hgrn_kernel(q_ref, ff_ref, fb_ref, i_ref, g_ref, lbf_ref, lbb_ref, gn_ref,
                 lvf_ref, lvb_ref, trif_ref, trib_ref, o_ref, of_ref, st_ref, *, chunk):
    S = q_ref.shape[0]
    n_chunks = S // chunk

    def chunk_inputs(c):
        rows = pl.ds(pl.multiple_of(c * chunk, chunk), chunk)
        qr = q_ref[rows, :]
        return rows, qr * _sigmoid(qr), i_ref[rows, :]

    st_ref[...] = jnp.zeros_like(st_ref)

    def fwd_body(c, carry):
        rows, q, v = chunk_inputs(c)
        of_ref[rows, :] = _gla_chunk(q, ff_ref[rows, :], v, lbf_ref[...], st_ref, lvf_ref, trif_ref, False)
        return carry

    lax.fori_loop(0, n_chunks, fwd_body, 0)
    st_ref[...] = jnp.zeros_like(st_ref)

    def bwd_body(ci, carry):
        rows, q, v = chunk_inputs(n_chunks - 1 - ci)
        o = of_ref[rows, :] + _gla_chunk(q, fb_ref[rows, :], v, lbb_ref[...], st_ref, lvb_ref, trib_ref, True)
        gate = g_ref[rows, :]
        o_ref[rows, :] = (_rms(o, gn_ref[...]) * (gate * _sigmoid(gate))).astype(o_ref.dtype)
        return carry

    lax.fori_loop(0, n_chunks, bwd_body, 0)


def _hgrn_call(proj, lb, layer, depth, gnorm, B, S, D):
    H = D // HEAD
    C = GLA_CHUNK
    col = lambda c: pl.BlockSpec((S, HEAD), lambda b, h, c=c: (b, c * H + h))
    lb_spec = lambda r: pl.BlockSpec((None, 1, HEAD), lambda b, h, r=r: (r, 0, h))
    const = lambda shape: pl.BlockSpec(shape, lambda b, h: (0, 0))
    return pl.pallas_call(
        functools.partial(_hgrn_kernel, chunk=C),
        grid=(B, H),
        in_specs=[col(0), col(1), col(2), col(3), col(4),
                  lb_spec(layer), lb_spec(depth + layer), const((1, HEAD)),
                  const((C, C)), const((C, C)), const((C, C)), const((C, C))],
        out_specs=pl.BlockSpec((S, HEAD), lambda b, h: (b, h)),
        out_shape=jax.ShapeDtypeStruct((B * S, D), BF16),
        scratch_shapes=[pltpu.VMEM((S, HEAD), F32), pltpu.VMEM((HEAD, HEAD), F32)],
        compiler_params=_cparams("parallel", "parallel"),
        name="hgrn2_mixer",
    )(proj, proj, proj, proj, proj, lb, lb, gnorm.reshape(1, HEAD),
      jnp.asarray(_gla_level_ids(C, False)), jnp.asarray(_gla_level_ids(C, True)),
      jnp.asarray(_gla_tri(C, False), BF16), jnp.asarray(_gla_tri(C, True), BF16))


def _rope_tables(S):
    pos = jnp.arange(S, dtype=F32)
    inv_freq = ROPE_THETA ** (-(jnp.arange(0, ROPE_DIM, 2, dtype=F32) / ROPE_DIM))
    ang = pos[:, None] * inv_freq[None, :]
    ang = jnp.concatenate([ang, ang], axis=-1)
    cos, sin = jnp.cos(ang), jnp.sin(ang)
    half = ROPE_DIM // 2
    pad = lambda t, val: jnp.concatenate([t, jnp.full((S, HEAD - t.shape[1]), val, F32)], axis=-1)
    zeros = jnp.zeros((S, half), F32)
    cos_t = pad(cos, 1.0)
    sin_a = pad(jnp.concatenate([-sin[:, :half], zeros], axis=-1), 0.0)
    sin_b = pad(jnp.concatenate([zeros, sin[:, half:]], axis=-1), 0.0)
    return cos_t, sin_a, sin_b


def _attn_group(q_ref, k_ref, v_ref, cos_ref, sa_ref, sb_ref, o_ref,
                qs_ref, ks_ref, vs_ref, m_ref, l_ref, acc_ref, *, d, first, last):
    S = q_ref.shape[0]
    L = S // d
    log_l = int(math.log2(L))
    half = ROPE_DIM // 2
    scale = HEAD ** -0.5

    def natural_rows(p0, n):
        if d == 1:
            return pl.ds(p0, n)
        return pl.ds((p0 >> log_l) + d * (p0 & (L - 1)), n, stride=d)

    def rope(x, src):
        return (x * cos_ref[src, :] + pltpu.roll(x, HEAD - half, 1) * sa_ref[src, :]
                + pltpu.roll(x, half, 1) * sb_ref[src, :])

    def stage_body(i, carry):
        p0 = pl.multiple_of(i * ATTN_P1, ATTN_P1)
        src = natural_rows(p0, ATTN_P1)
        dst = pl.ds(p0, ATTN_P1)
        qs_ref[dst, :] = (rope(q_ref[src, :], src) * scale).astype(BF16)
        ks_ref[dst, :] = rope(k_ref[src, :], src).astype(BF16)
        vs_ref[dst, :] = v_ref[src, :].astype(BF16)
        return carry

    lax.fori_loop(0, S // ATTN_P1, stage_body, 0)

    def block_body(i, carry):
        p0 = pl.multiple_of(i * ATTN_BQ, ATTN_BQ)
        k0 = pl.multiple_of(jnp.clip(p0 - ATTN_RADIUS, 0, S - ATTN_KW), ATTN_RADIUS)
        s = lax.dot_general(qs_ref[pl.ds(p0, ATTN_BQ), :], ks_ref[pl.ds(k0, ATTN_KW), :], NT_DIMS,
                            preferred_element_type=F32)
        qi = p0 + lax.broadcasted_iota(jnp.int32, s.shape, 0)
        ki = k0 + lax.broadcasted_iota(jnp.int32, s.shape, 1)
        in_band = jnp.abs(ki - qi) <= ATTN_RADIUS
        same_seq = (qi >> log_l) == (ki >> log_l)
        s = jnp.where(in_band, jnp.where(same_seq, s, NEG_INF), NEG_INF)
        m = jnp.max(s, axis=-1, keepdims=True)
        p = jnp.exp(s - m)
        l = jnp.sum(p, axis=-1, keepdims=True)
        acc = jnp.dot(p.astype(BF16), vs_ref[pl.ds(k0, ATTN_KW), :], preferred_element_type=F32)
        m = jnp.broadcast_to(m, acc.shape)
        l = jnp.broadcast_to(l, acc.shape)
        nat = natural_rows(p0, ATTN_BQ)
        if not first:
            m_old = m_ref[nat, :]
            m_new = jnp.maximum(m_old, m)
            a_old = jnp.exp(m_old - m_new)
            a_new = jnp.exp(m - m_new)
            l = a_old * l_ref[nat, :] + a_new * l
            acc = a_old * acc_ref[nat, :] + a_new * acc
            m = m_new
        if last:
            acc_ref[nat, :] = acc / l
        else:
            m_ref[nat, :] = m
            l_ref[nat, :] = l
            acc_ref[nat, :] = acc
        return carry

    lax.fori_loop(0, S // ATTN_BQ, block_body, 0)

    if last:
        def out_body(i, carry):
            rows = pl.ds(pl.multiple_of(i * ATTN_P1, ATTN_P1), ATTN_P1)
            o_ref[rows, :] = acc_ref[rows, :].astype(o_ref.dtype)
            return carry

        lax.fori_loop(0, S // ATTN_P1, out_body, 0)


def _attn_kernel(q_ref, k_ref, v_ref, cos_ref, sa_ref, sb_ref, o_ref,
                 qs_ref, ks_ref, vs_ref, m_ref, l_ref, acc_ref):
    g = pl.program_id(2)
    n_groups = len(DILATED_GROUPS)
    for gi, (window, dilation) in enumerate(DILATED_GROUPS):
        assert window // (2 * dilation) == ATTN_RADIUS

        @pl.when(g == gi)
        def _(gi=gi, dilation=dilation):
            _attn_group(q_ref, k_ref, v_ref, cos_ref, sa_ref, sb_ref, o_ref,
                        qs_ref, ks_ref, vs_ref, m_ref, l_ref, acc_ref,
                        d=dilation, first=gi == 0, last=gi == n_groups - 1)


def _attn_call(proj, tables, B, S, D):
    H = D // HEAD
    n_groups = len(DILATED_GROUPS)
    col = lambda j: pl.BlockSpec((S, HEAD), lambda b, h, g, j=j: (b, (g * 3 + j) * H + h))
    table = pl.BlockSpec((S, HEAD), lambda b, h, g: (0, 0))
    return pl.pallas_call(
        _attn_kernel,
        grid=(B, H, n_groups),
        in_specs=[col(0), col(1), col(2), table, table, table],
        out_specs=pl.BlockSpec((S, HEAD), lambda b, h, g: (b, h)),
        out_shape=jax.ShapeDtypeStruct((B * S, D), BF16),
        scratch_shapes=[pltpu.VMEM((S, HEAD), BF16), pltpu.VMEM((S, HEAD), BF16), pltpu.VMEM((S, HEAD), BF16),
                        pltpu.VMEM((S, HEAD), F32), pltpu.VMEM((S, HEAD), F32), pltpu.VMEM((S, HEAD), F32)],
        compiler_params=_cparams("parallel", "parallel", "arbitrary"),
        name="dilated_attention",
    )(proj, proj, proj, *tables)


def kernel(x, norm_w, w_in_hgrn, hgrn_lower_bounds, hgrn_gnorm, w_in_attn, w_out, w_ffn_in, w_ffn_out):
    B, S, D = x.shape
    depth = norm_w.shape[0]
    T = B * S
    xf = x.reshape(T, D).astype(F32)
    lb = _lb_call(hgrn_lower_bounds).reshape(2 * depth, 1, D)
    tables = _rope_tables(S)

    hn = _norm_call(xf, w_pre=norm_w[0, 0])
    for layer in range(depth):
        slot = layer // 2
        if layer % 2 == 0:
            proj = _matmul(hn, w_in_hgrn, slot, tm=1024, tn=1024, out_dtype=F32)
            mixed = _hgrn_call(proj, lb, layer, depth, hgrn_gnorm[slot], B, S, D)
        else:
            proj = _matmul(hn, w_in_attn, slot, tm=1024, tn=1024, out_dtype=F32)
            mixed = _attn_call(proj, tables, B, S, D)
        h = _matmul(mixed, w_out, layer, tm=1024, tn=1024, out_dtype=F32)
        xf, hn = _norm_call(xf, h, norm_w[layer, 1], norm_w[layer, 2])
        act = _matmul_swiglu(hn, w_ffn_in, layer, tm=1024, tn=512)
        h = _matmul(act, w_ffn_out, layer, tm=512, tn=512, out_dtype=F32)
        if layer + 1 < depth:
            xf, hn = _norm_call(xf, h, norm_w[layer, 3], norm_w[layer + 1, 0])
        else:
            xf = _norm_call(xf, h, norm_w[layer, 3])
    return xf.reshape(B, S, D).astype(x.dtype)
```

```python
import functools
import math

import numpy as np
import jax
import jax.numpy as jnp
from jax import lax
from jax.experimental import pallas as pl
from jax.experimental.pallas import tpu as pltpu

F32 = jnp.float32
BF16 = jnp.bfloat16

EPS = 1e-6
HEAD = 128
DILATED_GROUPS = ((128, 1), (512, 4), (2048, 16))
ROPE_THETA = 500000.0
ROPE_DIM = HEAD // 4
NEG_INF = -1e30

GLA_CHUNK = 128
ATTN_BQ = 128
ATTN_RADIUS = 64
ATTN_KW = ATTN_BQ + 2 * ATTN_RADIUS
ATTN_P1 = 256
ATTN_UNROLL = 4

VMEM_LIMIT = 56 * 1024 * 1024

NT_DIMS = (((1,), (1,)), ((), ()))
TN_DIMS = (((0,), (0,)), ((), ()))


def _cparams(*sem):
    return pltpu.CompilerParams(dimension_semantics=sem, vmem_limit_bytes=VMEM_LIMIT)


def _rms(x, w):
    return x * lax.rsqrt(jnp.mean(x * x, axis=-1, keepdims=True) + EPS) * w


def _sigmoid(x):
    return 1.0 / (1.0 + jnp.exp(-x))


def _norm_kernel(*refs, has_h, has_pre):
    refs = list(refs)
    x_ref = refs.pop(0)
    x = x_ref[...]
    if has_h:
        h_ref = refs.pop(0)
        wpost_ref = refs.pop(0)
    if has_pre:
        wpre_ref = refs.pop(0)
    if has_h:
        xo_ref = refs.pop(0)
        x = x + _rms(h_ref[...].astype(F32), wpost_ref[...])
        xo_ref[...] = x
    if has_pre:
        hn_ref = refs.pop(0)
        hn_ref[...] = _rms(x, wpre_ref[...]).astype(BF16)


def _norm_call(x, h=None, w_post=None, w_pre=None, rows=256):
    T, D = x.shape
    has_h, has_pre = h is not None, w_pre is not None
    row_spec = pl.BlockSpec((rows, D), lambda i: (i, 0))
    w_spec = pl.BlockSpec((1, D), lambda i: (0, 0))
    args, in_specs, out_shape, out_specs = [x], [row_spec], [], []
    if has_h:
        args += [h, w_post.reshape(1, D)]
        in_specs += [row_spec, w_spec]
        out_shape.append(jax.ShapeDtypeStruct((T, D), F32))
        out_specs.append(row_spec)
    if has_pre:
        args.append(w_pre.reshape(1, D))
        in_specs.append(w_spec)
        out_shape.append(jax.ShapeDtypeStruct((T, D), BF16))
        out_specs.append(row_spec)
    outs = pl.pallas_call(
        functools.partial(_norm_kernel, has_h=has_h, has_pre=has_pre),
        grid=(T // rows,),
        in_specs=in_specs,
        out_specs=out_specs,
        out_shape=out_shape,
        compiler_params=_cparams("parallel"),
        name="norm_merge",
    )(*args)
    return outs if len(outs) > 1 else outs[0]


def _mm_kernel(x_ref, w_ref, o_ref, wb_ref):
    @pl.when(pl.program_id(1) == 0)
    def _():
        wb_ref[...] = w_ref[...].astype(BF16)

    o_ref[...] = jnp.dot(x_ref[...], wb_ref[...], preferred_element_type=F32).astype(o_ref.dtype)


def _matmul(x, w, layer, *, tm, tn, out_dtype):
    T, K = x.shape
    N = w.shape[2]
    return pl.pallas_call(
        _mm_kernel,
        grid=(N // tn, T // tm),
        in_specs=[
            pl.BlockSpec((tm, K), lambda j, i: (i, 0)),
            pl.BlockSpec((None, K, tn), lambda j, i: (layer, 0, j)),
        ],
        out_specs=pl.BlockSpec((tm, tn), lambda j, i: (i, j)),
        out_shape=jax.ShapeDtypeStruct((T, N), out_dtype),
        scratch_shapes=[pltpu.VMEM((K, tn), BF16)],
        compiler_params=_cparams("parallel", "arbitrary"),
        name="matmul",
    )(x, w)


def _swiglu_kernel(x_ref, wg_ref, wu_ref, o_ref, wgb_ref, wub_ref):
    @pl.when(pl.program_id(1) == 0)
    def _():
        wgb_ref[...] = wg_ref[...].astype(BF16)
        wub_ref[...] = wu_ref[...].astype(BF16)

    x = x_ref[...]
    gate = jnp.dot(x, wgb_ref[...], preferred_element_type=F32)
    up = jnp.dot(x, wub_ref[...], preferred_element_type=F32)
    o_ref[...] = (gate * _sigmoid(gate) * up).astype(o_ref.dtype)


def _matmul_swiglu(x, w, layer, *, tm, tn):
    T, K = x.shape
    F = w.shape[2] // 2
    nf = F // tn
    return pl.pallas_call(
        _swiglu_kernel,
        grid=(nf, T // tm),
        in_specs=[
            pl.BlockSpec((tm, K), lambda j, i: (i, 0)),
            pl.BlockSpec((None, K, tn), lambda j, i: (layer, 0, j)),
            pl.BlockSpec((None, K, tn), lambda j, i: (layer, 0, j + nf)),
        ],
        out_specs=pl.BlockSpec((tm, tn), lambda j, i: (i, j)),
        out_shape=jax.ShapeDtypeStruct((T, F), BF16),
        scratch_shapes=[pltpu.VMEM((K, tn), BF16), pltpu.VMEM((K, tn), BF16)],
        compiler_params=_cparams("parallel", "arbitrary"),
        name="matmul_swiglu",
    )(x, w, w)


def _lb_kernel(x_ref, o_ref, *, depth):
    for d in range(2):
        r = [x_ref[pl.ds(d * depth + i, 1), :] for i in range(depth)]
        m = functools.reduce(jnp.maximum, r)
        e = [jnp.exp(v - m) for v in r]
        tot = functools.reduce(lambda a, b: a + b, e)
        c = None
        first = None
        for i in range(depth):
            p = e[i] / tot
            c = p if c is None else c + p
            if first is None:
                first = c
            o_ref[pl.ds(d * depth + i, 1), :] = c - first


def _lb_call(lower_bounds):
    two, depth, D = lower_bounds.shape
    return pl.pallas_call(
        functools.partial(_lb_kernel, depth=depth),
        out_shape=jax.ShapeDtypeStruct((two * depth, D), F32),
        name="hgrn_lower_bounds",
    )(lower_bounds.reshape(two * depth, D).astype(F32))


def _gla_level_ids(C, rev):
    t = np.arange(C)[:, None]
    s = np.arange(C)[None, :]
    x = t ^ s
    lv = np.where(x > 0, np.floor(np.log2(np.maximum(x, 1))).astype(np.int64), int(math.log2(C)))
    allowed = (t <= s) if rev else (t >= s)
    return np.where(allowed, lv, -1).astype(np.int32)


def _gla_tri(C, rev):
    t = np.arange(C)[:, None]
    j = np.arange(C)[None, :]
    return ((j >= t) if rev else (j <= t)).astype(np.float32)


def _mid_rows(b, w, rev, tmod):
    C = b.shape[0]
    span = 2 * w
    mid = w if rev else w - 1
    if span >= 8:
        pieces = []
        for a in range(C // span):
            row = b[a * span + mid:a * span + mid + 1, :]
            pieces.append(jnp.broadcast_to(row, (span, b.shape[1])))
        return pieces[0] if len(pieces) == 1 else jnp.concatenate(pieces, axis=0)
    g = b
    for o in range(span):
        delta = mid - o
        if delta != 0:
            g = jnp.where(tmod[w] == o, pltpu.roll(b, (-delta) % C, 0), g)
    return g


def _gla_chunk(q, f_raw, v, lb, st_ref, lv_ref, tri_ref, rev):
    C = q.shape[0]
    nlev = int(math.log2(C))
    fg = lb + (1.0 - lb) * _sigmoid(f_raw)
    k = 1.0 - fg
    lg = jnp.log(fg)
    hi = lg.astype(BF16)
    r1 = lg - hi.astype(F32)
    mid = r1.astype(BF16)
    lo = (r1 - mid.astype(F32)).astype(BF16)
    tri = tri_ref[...]
    b = (jnp.dot(tri, hi, preferred_element_type=F32)
         + jnp.dot(tri, mid, preferred_element_type=F32)
         + jnp.dot(tri, lo, preferred_element_type=F32))

    tio = lax.broadcasted_iota(jnp.int32, b.shape, 0)
    tmod = {1: tio & 1, 2: tio & 3}
    lv = lv_ref[...]
    qb = q.astype(BF16)
    kb = k.astype(BF16)
    vb = v.astype(BF16)
    scores = jnp.where(lv == nlev, lax.dot_general(qb, kb, NT_DIMS, preferred_element_type=F32), 0.0)
    for p in range(nlev):
        w = 1 << p
        z = jnp.exp(-jnp.abs(b - _mid_rows(b, w, rev, tmod)))
        s = lax.dot_general((q * z).astype(BF16), (k * z).astype(BF16), NT_DIMS,
                            preferred_element_type=F32)
        scores = jnp.where(lv == p, s, scores)

    b_edge = b[0:1, :] if rev else b[C - 1:C, :]
    q_st = (q * jnp.exp(b)).astype(BF16)
    k_st = (k * jnp.exp(b_edge - b)).astype(BF16)
    st = st_ref[...]
    o = (jnp.dot(scores.astype(BF16), vb, preferred_element_type=F32)
         + lax.dot_general(q_st, st.astype(BF16), NT_DIMS, preferred_element_type=F32))
    st_ref[...] = st * jnp.exp(b_edge) + lax.dot_general(vb, k_st, TN_DIMS, preferred_element_type=F32)
    return o


def _hgrn_kernel(q_ref, ff_ref, fb_ref, i_ref, g_ref, lbf_ref, lbb_ref, gn_ref,
                 lvf_ref, lvb_ref, trif_ref, trib_ref, o_ref, of_ref, st_ref, *, chunk):
    S = q_ref.shape[0]
    n_chunks = S // chunk

    def chunk_inputs(c):
        rows = pl.ds(pl.multiple_of(c * chunk, chunk), chunk)
        qr = q_ref[rows, :]
        return rows, qr * _sigmoid(qr), i_ref[rows, :]

    st_ref[...] = jnp.zeros_like(st_ref)

    def fwd_body(c, carry):
        rows, q, v = chunk_inputs(c)
        of_ref[rows, :] = _gla_chunk(q, ff_ref[rows, :], v, lbf_ref[...], st_ref, lvf_ref, trif_ref, False)
        return carry

    lax.fori_loop(0, n_chunks, fwd_body, 0)
    st_ref[...] = jnp.zeros_like(st_ref)

    def bwd_body(ci, carry):
        rows, q, v = chunk_inputs(n_chunks - 1 - ci)
        o = of_ref[rows, :] + _gla_chunk(q, fb_ref[rows, :], v, lbb_ref[...], st_ref, lvb_ref, trib_ref, True)
        gate = g_ref[rows, :]
        o_ref[rows, :] = (_rms(o, gn_ref[...]) * (gate * _sigmoid(gate))).astype(o_ref.dtype)
        return carry

    lax.fori_loop(0, n_chunks, bwd_body, 0)


def _hgrn_call(proj, lb, layer, depth, gnorm, B, S, D):
    H = D // HEAD
    C = GLA_CHUNK
    col = lambda c: pl.BlockSpec((S, HEAD), lambda b, h, c=c: (b, c * H + h))
    lb_spec = lambda r: pl.BlockSpec((None, 1, HEAD), lambda b, h, r=r: (r, 0, h))
    const = lambda shape: pl.BlockSpec(shape, lambda b, h: (0, 0))
    return pl.pallas_call(
        functools.partial(_hgrn_kernel, chunk=C),
        grid=(B, H),
        in_specs=[col(0), col(1), col(2), col(3), col(4),
                  lb_spec(layer), lb_spec(depth + layer), const((1, HEAD)),
                  const((C, C)), const((C, C)), const((C, C)), const((C, C))],
        out_specs=pl.BlockSpec((S, HEAD), lambda b, h: (b, h)),
        out_shape=jax.ShapeDtypeStruct((B * S, D), BF16),
        scratch_shapes=[pltpu.VMEM((S, HEAD), F32), pltpu.VMEM((HEAD, HEAD), F32)],
        compiler_params=_cparams("parallel", "parallel"),
        name="hgrn2_mixer",
    )(proj, proj, proj, proj, proj, lb, lb, gnorm.reshape(1, HEAD),
      jnp.asarray(_gla_level_ids(C, False)), jnp.asarray(_gla_level_ids(C, True)),
      jnp.asarray(_gla_tri(C, False), BF16), jnp.asarray(_gla_tri(C, True), BF16))


def _rope_tables(S):
    pos = jnp.arange(S, dtype=F32)
    inv_freq = ROPE_THETA ** (-(jnp.arange(0, ROPE_DIM, 2, dtype=F32) / ROPE_DIM))
    ang = pos[:, None] * inv_freq[None, :]
    ang = jnp.concatenate([ang, ang], axis=-1)
    pad = lambda t, val: jnp.concatenate([t, jnp.full((S, HEAD - t.shape[1]), val, F32)], axis=-1)
    cos_t, sin_t = pad(jnp.cos(ang), 1.0), pad(jnp.sin(ang), 0.0)
    qs = HEAD ** -0.5 * math.log2(math.e)
    per_group = []
    for _, d in DILATED_GROUPS:
        order = lambda t: t.reshape(S // d, d, HEAD).transpose(1, 0, 2).reshape(S, HEAD)
        per_group.append(jnp.stack([order(cos_t) * qs, order(sin_t) * qs, order(cos_t), order(sin_t)]))
    return jnp.stack(per_group)


def _rotate_half_matrix():
    half = ROPE_DIM // 2
    p = np.zeros((HEAD, HEAD), np.float32)
    for l in range(half):
        p[l + half, l] = -1.0
        p[l, l + half] = 1.0
    return p


def _attn_bias():
    i = np.arange(ATTN_BQ)[:, None]
    j = np.arange(ATTN_KW)[None, :]
    band = lambda off: np.abs(j - i - off) <= ATTN_RADIUS
    r = ATTN_RADIUS
    masks = [band(0), band(r), band(r) & (j >= r), band(r) & (j < r + ATTN_BQ), band(2 * r)]
    return np.where(np.stack(masks), 0.0, NEG_INF).astype(np.float32)


def _attn_group(q_ref, k_ref, v_ref, tab_ref, rot_ref, bias_ref, o_ref,
                qs_ref, ks_ref, vs_ref, m_ref, l_ref, acc_ref, *, d, first, last):
    S = q_ref.shape[0]
    L = S // d
    assert L >= ATTN_KW and L % ATTN_BQ == 0
    log_l = int(math.log2(L))

    def natural_rows(p0, n):
        if d == 1:
            return pl.ds(p0, n)
        return pl.ds((p0 >> log_l) + d * (p0 & (L - 1)), n, stride=d)

    def rope(x, cos, sin):
        rot = jnp.dot(x.astype(BF16), rot_ref[...], preferred_element_type=F32)
        return (x * cos + rot * sin).astype(BF16)

    def stage_body(i, carry):
        p0 = pl.multiple_of(i * ATTN_P1, ATTN_P1)
        src = natural_rows(p0, ATTN_P1)
        dst = pl.ds(p0, ATTN_P1)
        qs_ref[dst, :] = rope(q_ref[src, :], tab_ref[0, dst, :], tab_ref[1, dst, :])
        ks_ref[dst, :] = rope(k_ref[src, :], tab_ref[2, dst, :], tab_ref[3, dst, :])
        vs_ref[dst, :] = v_ref[src, :].astype(BF16)
        return carry

    lax.fori_loop(0, S // ATTN_P1, stage_body, 0)

    def one_block(blk):
        p0 = pl.multiple_of(blk * ATTN_BQ, ATTN_BQ)
        k0 = pl.multiple_of(jnp.clip(p0 - ATTN_RADIUS, 0, S - ATTN_KW), ATTN_RADIUS)
        at_start = ((p0 & (L - 1)) == 0).astype(jnp.int32)
        at_end = (((p0 + ATTN_BQ) & (L - 1)) == 0).astype(jnp.int32)
        geom = jnp.where(p0 == 0, 0, jnp.where(p0 == S - ATTN_BQ, 4, 1 + at_start + 2 * at_end))
        s = lax.dot_general(qs_ref[pl.ds(p0, ATTN_BQ), :], ks_ref[pl.ds(k0, ATTN_KW), :], NT_DIMS,
                            preferred_element_type=F32)
        s = s + bias_ref[geom]
        m = jnp.max(s, axis=-1, keepdims=True)
        p = jnp.exp2(s - m)
        l = jnp.sum(p, axis=-1, keepdims=True)
        acc = jnp.dot(p.astype(BF16), vs_ref[pl.ds(k0, ATTN_KW), :], preferred_element_type=F32)
        m = jnp.broadcast_to(m, acc.shape)
        l = jnp.broadcast_to(l, acc.shape)
        nat = natural_rows(p0, ATTN_BQ)
        if not first:
            m_old = m_ref[nat, :]
            m_new = jnp.maximum(m_old, m)
            a_old = jnp.exp2(m_old - m_new)
            a_new = jnp.exp2(m - m_new)
            l = a_old * l_ref[nat, :] + a_new * l
            acc = a_old * acc_ref[nat, :] + a_new * acc
            m = m_new
        if last:
            acc_ref[nat, :] = acc / l
        else:
            m_ref[nat, :] = m
            l_ref[nat, :] = l
            acc_ref[nat, :] = acc

    def block_body(i, carry):
        for u in range(ATTN_UNROLL):
            one_block(i * ATTN_UNROLL + u)
        return carry

    lax.fori_loop(0, S // (ATTN_BQ * ATTN_UNROLL), block_body, 0)

    if last:
        def out_body(i, carry):
            rows = pl.ds(pl.multiple_of(i * ATTN_P1, ATTN_P1), ATTN_P1)
            o_ref[rows, :] = acc_ref[rows, :].astype(o_ref.dtype)
            return carry

        lax.fori_loop(0, S // ATTN_P1, out_body, 0)


def _attn_kernel(q_ref, k_ref, v_ref, tab_ref, rot_ref, bias_ref, o_ref,
                 qs_ref, ks_ref, vs_ref, m_ref, l_ref, acc_ref):
    g = pl.program_id(2)
    n_groups = len(DILATED_GROUPS)
    for gi, (window, dilation) in enumerate(DILATED_GROUPS):
        assert window // (2 * dilation) == ATTN_RADIUS

        @pl.when(g == gi)
        def _(gi=gi, dilation=dilation):
            _attn_group(q_ref, k_ref, v_ref, tab_ref, rot_ref, bias_ref, o_ref,
                        qs_ref, ks_ref, vs_ref, m_ref, l_ref, acc_ref,
                        d=dilation, first=gi == 0, last=gi == n_groups - 1)


def _attn_call(proj, tables, B, S, D):
    H = D // HEAD
    n_groups = len(DILATED_GROUPS)
    col = lambda j: pl.BlockSpec((S, HEAD), lambda b, h, g, j=j: (b, (g * 3 + j) * H + h))
    return pl.pallas_call(
        _attn_kernel,
        grid=(B, H, n_groups),
        in_specs=[col(0), col(1), col(2),
                  pl.BlockSpec((None, 4, S, HEAD), lambda b, h, g: (g, 0, 0, 0)),
                  pl.BlockSpec((HEAD, HEAD), lambda b, h, g: (0, 0)),
                  pl.BlockSpec((5, ATTN_BQ, ATTN_KW), lambda b, h, g: (0, 0, 0))],
        out_specs=pl.BlockSpec((S, HEAD), lambda b, h, g: (b, h)),
        out_shape=jax.ShapeDtypeStruct((B * S, D), BF16),
        scratch_shapes=[pltpu.VMEM((S, HEAD), BF16), pltpu.VMEM((S, HEAD), BF16), pltpu.VMEM((S, HEAD), BF16),
                        pltpu.VMEM((S, HEAD), F32), pltpu.VMEM((S, HEAD), F32), pltpu.VMEM((S, HEAD), F32)],
        compiler_params=_cparams("parallel", "parallel", "arbitrary"),
        name="dilated_attention",
    )(proj, proj, proj, tables, jnp.asarray(_rotate_half_matrix(), BF16), jnp.asarray(_attn_bias()))


def kernel(x, norm_w, w_in_hgrn, hgrn_lower_bounds, hgrn_gnorm, w_in_attn, w_out, w_ffn_in, w_ffn_out):
    B, S, D = x.shape
    depth = norm_w.shape[0]
    T = B * S
    xf = x.reshape(T, D).astype(F32)
    lb = _lb_call(hgrn_lower_bounds).reshape(2 * depth, 1, D)
    tables = _rope_tables(S)

    hn = _norm_call(xf, w_pre=norm_w[0, 0])
    for layer in range(depth):
        slot = layer // 2
        if layer % 2 == 0:
            proj = _matmul(hn, w_in_hgrn, slot, tm=1024, tn=1024, out_dtype=F32)
            mixed = _hgrn_call(proj, lb, layer, depth, hgrn_gnorm[slot], B, S, D)
        else:
            proj = _matmul(hn, w_in_attn, slot, tm=1024, tn=1024, out_dtype=F32)
            mixed = _attn_call(proj, tables, B, S, D)
        h = _matmul(mixed, w_out, layer, tm=1024, tn=1024, out_dtype=F32)
        xf, hn = _norm_call(xf, h, norm_w[layer, 1], norm_w[layer, 2])
        act = _matmul_swiglu(hn, w_ffn_in, layer, tm=1024, tn=512)
        h = _matmul(act, w_ffn_out, layer, tm=512, tn=512, out_dtype=F32)
        if layer + 1 < depth:
            xf, hn = _norm_call(xf, h, norm_w[layer, 3], norm_w[layer + 1, 0])
        else:
            xf = _norm_call(xf, h, norm_w[layer, 3])
    return xf.reshape(B, S, D).astype(x.dtype)
```

```python
import functools
import math

import numpy as np
import jax
import jax.numpy as jnp
from jax import lax
from jax.experimental import pallas as pl
from jax.experimental.pallas import tpu as pltpu

F32 = jnp.float32
BF16 = jnp.bfloat16

EPS = 1e-6
HEAD = 128
DILATED_GROUPS = ((128, 1), (512, 4), (2048, 16))
ROPE_THETA = 500000.0
ROPE_DIM = HEAD // 4
NEG_INF = -1e30
LOG2E = math.log2(math.e)

GLA_CHUNK = 128
ATTN_BQ = 128
ATTN_RADIUS = 64
ATTN_KW = ATTN_BQ + 2 * ATTN_RADIUS
ATTN_P1 = 256
ATTN_UNROLL = 8

VMEM_LIMIT = 56 * 1024 * 1024

NT_DIMS = (((1,), (1,)), ((), ()))
TN_DIMS = (((0,), (0,)), ((), ()))


def _cparams(*sem):
    return pltpu.CompilerParams(dimension_semantics=sem, vmem_limit_bytes=VMEM_LIMIT)


def _rms(x, w):
    return x * lax.rsqrt(jnp.mean(x * x, axis=-1, keepdims=True) + EPS) * w


def _sigmoid(x):
    return 1.0 / (1.0 + jnp.exp(-x))


def _norm_kernel(*refs, has_h, has_pre):
    refs = list(refs)
    x_ref = refs.pop(0)
    x = x_ref[...]
    if has_h:
        h_ref = refs.pop(0)
        wpost_ref = refs.pop(0)
    if has_pre:
        wpre_ref = refs.pop(0)
    if has_h:
        xo_ref = refs.pop(0)
        x = x + _rms(h_ref[...].astype(F32), wpost_ref[...])
        xo_ref[...] = x
    if has_pre:
        hn_ref = refs.pop(0)
        hn_ref[...] = _rms(x, wpre_ref[...]).astype(BF16)


def _norm_call(x, h=None, w_post=None, w_pre=None, rows=256):
    T, D = x.shape
    has_h, has_pre = h is not None, w_pre is not None
    row_spec = pl.BlockSpec((rows, D), lambda i: (i, 0))
    w_spec = pl.BlockSpec((1, D), lambda i: (0, 0))
    args, in_specs, out_shape, out_specs = [x], [row_spec], [], []
    if has_h:
        args += [h, w_post.reshape(1, D)]
        in_specs += [row_spec, w_spec]
        out_shape.append(jax.ShapeDtypeStruct((T, D), F32))
        out_specs.append(row_spec)
    if has_pre:
        args.append(w_pre.reshape(1, D))
        in_specs.append(w_spec)
        out_shape.append(jax.ShapeDtypeStruct((T, D), BF16))
        out_specs.append(row_spec)
    outs = pl.pallas_call(
        functools.partial(_norm_kernel, has_h=has_h, has_pre=has_pre),
        grid=(T // rows,),
        in_specs=in_specs,
        out_specs=out_specs,
        out_shape=out_shape,
        compiler_params=_cparams("parallel"),
        name="norm_merge",
    )(*args)
    return outs if len(outs) > 1 else outs[0]


def _mm_kernel(x_ref, w_ref, o_ref, wb_ref):
    @pl.when(pl.program_id(1) == 0)
    def _():
        wb_ref[...] = w_ref[...].astype(BF16)

    o_ref[...] = jnp.dot(x_ref[...], wb_ref[...], preferred_element_type=F32).astype(o_ref.dtype)


def _matmul(x, w, layer, *, tm, tn, out_dtype):
    T, K = x.shape
    N = w.shape[2]
    return pl.pallas_call(
        _mm_kernel,
        grid=(N // tn, T // tm),
        in_specs=[
            pl.BlockSpec((tm, K), lambda j, i: (i, 0)),
            pl.BlockSpec((None, K, tn), lambda j, i: (layer, 0, j)),
        ],
        out_specs=pl.BlockSpec((tm, tn), lambda j, i: (i, j)),
        out_shape=jax.ShapeDtypeStruct((T, N), out_dtype),
        scratch_shapes=[pltpu.VMEM((K, tn), BF16)],
        compiler_params=_cparams("parallel", "arbitrary"),
        name="matmul",
    )(x, w)


def _swiglu_kernel(x_ref, wg_ref, wu_ref, o_ref, wgb_ref, wub_ref):
    @pl.when(pl.program_id(1) == 0)
    def _():
        wgb_ref[...] = wg_ref[...].astype(BF16)
        wub_ref[...] = wu_ref[...].astype(BF16)

    x = x_ref[...]
    gate = jnp.dot(x, wgb_ref[...], preferred_element_type=F32)
    up = jnp.dot(x, wub_ref[...], preferred_element_type=F32)
    o_ref[...] = (gate * _sigmoid(gate) * up).astype(o_ref.dtype)


def _matmul_swiglu(x, w, layer, *, tm, tn):
    T, K = x.shape
    F = w.shape[2] // 2
    nf = F // tn
    return pl.pallas_call(
        _swiglu_kernel,
        grid=(nf, T // tm),
        in_specs=[
            pl.BlockSpec((tm, K), lambda j, i: (i, 0)),
            pl.BlockSpec((None, K, tn), lambda j, i: (layer, 0, j)),
            pl.BlockSpec((None, K, tn), lambda j, i: (layer, 0, j + nf)),
        ],
        out_specs=pl.BlockSpec((tm, tn), lambda j, i: (i, j)),
        out_shape=jax.ShapeDtypeStruct((T, F), BF16),
        scratch_shapes=[pltpu.VMEM((K, tn), BF16), pltpu.VMEM((K, tn), BF16)],
        compiler_params=_cparams("parallel", "arbitrary"),
        name="matmul_swiglu",
    )(x, w, w)


def _lb_kernel(x_ref, o_ref, *, depth):
    for d in range(2):
        r = [x_ref[pl.ds(d * depth + i, 1), :] for i in range(depth)]
        m = functools.reduce(jnp.maximum, r)
        e = [jnp.exp(v - m) for v in r]
        tot = functools.reduce(lambda a, b: a + b, e)
        c = None
        first = None
        for i in range(depth):
            p = e[i] / tot
            c = p if c is None else c + p
            if first is None:
                first = c
            o_ref[pl.ds(d * depth + i, 1), :] = c - first


def _lb_call(lower_bounds):
    two, depth, D = lower_bounds.shape
    return pl.pallas_call(
        functools.partial(_lb_kernel, depth=depth),
        out_shape=jax.ShapeDtypeStruct((two * depth, D), F32),
        name="hgrn_lower_bounds",
    )(lower_bounds.reshape(two * depth, D).astype(F32))


def _gla_level_ids(C, rev):
    t = np.arange(C)[:, None]
    s = np.arange(C)[None, :]
    x = t ^ s
    lv = np.where(x > 0, np.floor(np.log2(np.maximum(x, 1))).astype(np.int64), int(math.log2(C)))
    allowed = (t <= s) if rev else (t >= s)
    return np.where(allowed, lv, -1).astype(np.int32)


def _gla_tri(C, rev):
    t = np.arange(C)[:, None]
    j = np.arange(C)[None, :]
    return ((j >= t) if rev else (j <= t)).astype(np.float32)


def _mid_rows(b, w, rev):
    C = b.shape[0]
    span = 2 * w
    mid = w if rev else w - 1
    pieces = []
    for a in range(C // span):
        row = b[a * span + mid:a * span + mid + 1, :]
        pieces.append(jnp.broadcast_to(row, (span, b.shape[1])))
    return pieces[0] if len(pieces) == 1 else jnp.concatenate(pieces, axis=0)


def _neg_abs(x):
    return lax.bitcast_convert_type(lax.bitcast_convert_type(x, jnp.uint32) | jnp.uint32(0x80000000), F32)


def _fine_decay(fg, w, rev):
    C = fg.shape[0]
    t4 = lax.broadcasted_iota(jnp.int32, fg.shape, 0) & 3
    if w == 1:
        keep = (t4 & 1) == (0 if rev else 1)
        return jnp.where(keep, fg, 1.0)
    nxt = pltpu.roll(fg, C - 1, 0)
    prv = pltpu.roll(fg, 1, 0)
    by_row = [fg * nxt, fg, 1.0, prv] if rev else [nxt, 1.0, fg, fg * prv]
    return jnp.where(t4 == 0, by_row[0], jnp.where(t4 == 1, by_row[1], jnp.where(t4 == 2, by_row[2], by_row[3])))


def _gla_chunk(qb, f_raw, vb, lb, st_ref, lv_ref, tri_ref, rev):
    C = qb.shape[0]
    nlev = int(math.log2(C))
    fg = lb + (1.0 - lb) * _sigmoid(f_raw)
    kb = (1.0 - fg).astype(BF16)
    lg = jnp.log(fg) * LOG2E
    hi = lg.astype(BF16)
    lo = (lg - hi.astype(F32)).astype(BF16)
    tri = tri_ref[...]
    b = jnp.dot(tri, hi, preferred_element_type=F32) + jnp.dot(tri, lo, preferred_element_type=F32)

    lv = lv_ref[...]
    scores = jnp.where(lv == nlev, lax.dot_general(qb, kb, NT_DIMS, preferred_element_type=F32), 0.0)
    for p in range(nlev):
        w = 1 << p
        if w < 4:
            z = _fine_decay(fg, w, rev).astype(BF16)
        else:
            z = jnp.exp2(_neg_abs(b - _mid_rows(b, w, rev))).astype(BF16)
        s = lax.dot_general(qb * z, kb * z, NT_DIMS, preferred_element_type=F32)
        scores = jnp.where(lv == p, s, scores)

    b_edge = b[0:1, :] if rev else b[C - 1:C, :]
    q_st = qb * jnp.exp2(b).astype(BF16)
    k_st = kb * jnp.exp2(b_edge - b).astype(BF16)
    st = st_ref[...]
    o = (jnp.dot(scores.astype(BF16), vb, preferred_element_type=F32)
         + lax.dot_general(q_st, st.astype(BF16), NT_DIMS, preferred_element_type=F32))
    st_ref[...] = st * jnp.exp2(b_edge) + lax.dot_general(vb, k_st, TN_DIMS, preferred_element_type=F32)
    return o


def _hgrn_kernel(q_ref, ff_ref, fb_ref, i_ref, g_ref, lbf_ref, lbb_ref, gn_ref,
                 lvf_ref, lvb_ref, trif_ref, trib_ref, o_ref,
                 qs_ref, vs_ref, of_ref, ob_ref, stf_ref, stb_ref, *, chunk):
    S = q_ref.shape[0]
    n_chunks = S // chunk

    def chunk_rows(c):
        return pl.ds(pl.multiple_of(c * chunk, chunk), chunk)

    def stage_body(c, carry):
        rows = chunk_rows(c)
        qr = q_ref[rows, :]
        qs_ref[rows, :] = (qr * _sigmoid(qr)).astype(BF16)
        vs_ref[rows, :] = i_ref[rows, :].astype(BF16)
        return carry

    lax.fori_loop(0, n_chunks, stage_body, 0)
    stf_ref[...] = jnp.zeros_like(stf_ref)
    stb_ref[...] = jnp.zeros_like(stb_ref)

    def scan_body(j, carry):
        rf = chunk_rows(j)
        rb = chunk_rows(n_chunks - 1 - j)
        of_ref[rf, :] = _gla_chunk(qs_ref[rf, :], ff_ref[rf, :], vs_ref[rf, :], lbf_ref[...],
                                   stf_ref, lvf_ref, trif_ref, False)
        ob_ref[rb, :] = _gla_chunk(qs_ref[rb, :], fb_ref[rb, :], vs_ref[rb, :], lbb_ref[...],
                                   stb_ref, lvb_ref, trib_ref, True)
        return carry

    lax.fori_loop(0, n_chunks, scan_body, 0, unroll=2)

    def out_body(c, carry):
        rows = chunk_rows(c)
        o = of_ref[rows, :] + ob_ref[rows, :]
        gate = g_ref[rows, :]
        o_ref[rows, :] = (_rms(o, gn_ref[...]) * (gate * _sigmoid(gate))).astype(o_ref.dtype)
        return carry

    lax.fori_loop(0, n_chunks, out_body, 0)


def _hgrn_call(proj, lb, layer, depth, gnorm, B, S, D):
    H = D // HEAD
    C = GLA_CHUNK
    col = lambda c: pl.BlockSpec((S, HEAD), lambda b, h, c=c: (b, c * H + h))
    lb_spec = lambda r: pl.BlockSpec((None, 1, HEAD), lambda b, h, r=r: (r, 0, h))
    const = lambda shape: pl.BlockSpec(shape, lambda b, h: (0, 0))
    return pl.pallas_call(
        functools.partial(_hgrn_kernel, chunk=C),
        grid=(B, H),
        in_specs=[col(0), col(1), col(2), col(3), col(4),
                  lb_spec(layer), lb_spec(depth + layer), const((1, HEAD)),
                  const((C, C)), const((C, C)), const((C, C)), const((C, C))],
        out_specs=pl.BlockSpec((S, HEAD), lambda b, h: (b, h)),
        out_shape=jax.ShapeDtypeStruct((B * S, D), BF16),
        scratch_shapes=[pltpu.VMEM((S, HEAD), BF16), pltpu.VMEM((S, HEAD), BF16),
                        pltpu.VMEM((S, HEAD), F32), pltpu.VMEM((S, HEAD), F32),
                        pltpu.VMEM((HEAD, HEAD), F32), pltpu.VMEM((HEAD, HEAD), F32)],
        compiler_params=_cparams("parallel", "parallel"),
        name="hgrn2_mixer",
    )(proj, proj, proj, proj, proj, lb, lb, gnorm.reshape(1, HEAD),
      jnp.asarray(_gla_level_ids(C, False)), jnp.asarray(_gla_level_ids(C, True)),
      jnp.asarray(_gla_tri(C, False), BF16), jnp.asarray(_gla_tri(C, True), BF16))


def _rope_tables(S):
    pos = jnp.arange(S, dtype=F32)
    inv_freq = ROPE_THETA ** (-(jnp.arange(0, ROPE_DIM, 2, dtype=F32) / ROPE_DIM))
    ang = pos[:, None] * inv_freq[None, :]
    ang = jnp.concatenate([ang, ang], axis=-1)
    pad = lambda t, val: jnp.concatenate([t, jnp.full((S, HEAD - t.shape[1]), val, F32)], axis=-1)
    cos_t, sin_t = pad(jnp.cos(ang), 1.0), pad(jnp.sin(ang), 0.0)
    qs = HEAD ** -0.5 * math.log2(math.e)
    per_group = []
    for _, d in DILATED_GROUPS:
        order = lambda t: t.reshape(S // d, d, HEAD).transpose(1, 0, 2).reshape(S, HEAD)
        per_group.append(jnp.stack([order(cos_t) * qs, order(sin_t) * qs, order(cos_t), order(sin_t)]))
    return jnp.stack(per_group)


def _rotate_half_matrix():
    half = ROPE_DIM // 2
    p = np.zeros((HEAD, HEAD), np.float32)
    for l in range(half):
        p[l + half, l] = -1.0
        p[l, l + half] = 1.0
    return p


def _attn_bias():
    i = np.arange(ATTN_BQ)[:, None]
    j = np.arange(ATTN_KW)[None, :]
    band = lambda off: np.abs(j - i - off) <= ATTN_RADIUS
    r = ATTN_RADIUS
    masks = [band(0), band(r), band(r) & (j >= r), band(r) & (j < r + ATTN_BQ), band(2 * r)]
    return np.where(np.stack(masks), 0.0, NEG_INF).astype(np.float32)


def _attn_group(q_ref, k_ref, v_ref, tab_ref, rot_ref, bias_ref, o_ref,
                qs_ref, ks_ref, vs_ref, m_ref, l_ref, acc_ref, *, d, first, last):
    S = q_ref.shape[0]
    L = S // d
    assert L >= ATTN_KW and L % ATTN_BQ == 0
    log_l = int(math.log2(L))

    def natural_rows(p0, n):
        if d == 1:
            return pl.ds(p0, n)
        return pl.ds((p0 >> log_l) + d * (p0 & (L - 1)), n, stride=d)

    def rope(x, cos, sin):
        rot = jnp.dot(x.astype(BF16), rot_ref[...], preferred_element_type=F32)
        return (x * cos + rot * sin).astype(BF16)

    def stage_body(i, carry):
        p0 = pl.multiple_of(i * ATTN_P1, ATTN_P1)
        src = natural_rows(p0, ATTN_P1)
        dst = pl.ds(p0, ATTN_P1)
        qs_ref[dst, :] = rope(q_ref[src, :], tab_ref[0, dst, :], tab_ref[1, dst, :])
        ks_ref[dst, :] = rope(k_ref[src, :], tab_ref[2, dst, :], tab_ref[3, dst, :])
        vs_ref[dst, :] = v_ref[src, :].astype(BF16)
        return carry

    lax.fori_loop(0, S // ATTN_P1, stage_body, 0, unroll=4)

    def one_block(blk):
        p0 = pl.multiple_of(blk * ATTN_BQ, ATTN_BQ)
        k0 = pl.multiple_of(jnp.clip(p0 - ATTN_RADIUS, 0, S - ATTN_KW), ATTN_RADIUS)
        at_start = ((p0 & (L - 1)) == 0).astype(jnp.int32)
        at_end = (((p0 + ATTN_BQ) & (L - 1)) == 0).astype(jnp.int32)
        geom = jnp.where(p0 == 0, 0, jnp.where(p0 == S - ATTN_BQ, 4, 1 + at_start + 2 * at_end))
        s = lax.dot_general(qs_ref[pl.ds(p0, ATTN_BQ), :], ks_ref[pl.ds(k0, ATTN_KW), :], NT_DIMS,
                            preferred_element_type=F32)
        s = s + bias_ref[geom]
        m = jnp.max(s, axis=-1, keepdims=True)
        p = jnp.exp2(s - m)
        l = jnp.sum(p, axis=-1, keepdims=True)
        acc = jnp.dot(p.astype(BF16), vs_ref[pl.ds(k0, ATTN_KW), :], preferred_element_type=F32)
        m = jnp.broadcast_to(m, acc.shape)
        l = jnp.broadcast_to(l, acc.shape)
        nat = natural_rows(p0, ATTN_BQ)
        if not first:
            m_old = m_ref[nat, :]
            m_new = jnp.maximum(m_old, m)
            a_old = jnp.exp2(m_old - m_new)
            a_new = jnp.exp2(m - m_new)
            l = a_old * l_ref[nat, :] + a_new * l
            acc = a_old * acc_ref[nat, :] + a_new * acc
            m = m_new
        if last:
            o_ref[nat, :] = (acc / l).astype(o_ref.dtype)
        else:
            m_ref[nat, :] = m
            l_ref[nat, :] = l
            acc_ref[nat, :] = acc

    unroll = ATTN_UNROLL * 2 if d == 1 else ATTN_UNROLL

    def block_body(i, carry):
        for u in range(unroll):
            one_block(i * unroll + u)
        return carry

    lax.fori_loop(0, S // (ATTN_BQ * unroll), block_body, 0)


def _attn_kernel(q_ref, k_ref, v_ref, tab_ref, rot_ref, bias_ref, o_ref,
                 qs_ref, ks_ref, vs_ref, m_ref, l_ref, acc_ref):
    g = pl.program_id(2)
    n_groups = len(DILATED_GROUPS)
    assert DILATED_GROUPS[0][1] == 1
    for step in range(n_groups):
        window, dilation = DILATED_GROUPS[n_groups - 1 - step]
        assert window // (2 * dilation) == ATTN_RADIUS

        @pl.when(g == step)
        def _(step=step, dilation=dilation):
            _attn_group(q_ref, k_ref, v_ref, tab_ref, rot_ref, bias_ref, o_ref,
                        qs_ref, ks_ref, vs_ref, m_ref, l_ref, acc_ref,
                        d=dilation, first=step == 0, last=step == n_groups - 1)


def _attn_call(proj, tables, B, S, D):
    H = D // HEAD
    n_groups = len(DILATED_GROUPS)
    col = lambda j: pl.BlockSpec((S, HEAD), lambda b, h, g, j=j: (b, ((n_groups - 1 - g) * 3 + j) * H + h))
    return pl.pallas_call(
        _attn_kernel,
        grid=(B, H, n_groups),
        in_specs=[col(0), col(1), col(2),
                  pl.BlockSpec((None, 4, S, HEAD), lambda b, h, g: (n_groups - 1 - g, 0, 0, 0)),
                  pl.BlockSpec((HEAD, HEAD), lambda b, h, g: (0, 0)),
                  pl.BlockSpec((5, ATTN_BQ, ATTN_KW), lambda b, h, g: (0, 0, 0))],
        out_specs=pl.BlockSpec((S, HEAD), lambda b, h, g: (b, h)),
        out_shape=jax.ShapeDtypeStruct((B * S, D), BF16),
        scratch_shapes=[pltpu.VMEM((S, HEAD), BF16), pltpu.VMEM((S, HEAD), BF16), pltpu.VMEM((S, HEAD), BF16),
                        pltpu.VMEM((S, HEAD), F32), pltpu.VMEM((S, HEAD), F32), pltpu.VMEM((S, HEAD), F32)],
        compiler_params=_cparams("parallel", "parallel", "arbitrary"),
        name="dilated_attention",
    )(proj, proj, proj, tables, jnp.asarray(_rotate_half_matrix(), BF16), jnp.asarray(_attn_bias()))


def kernel(x, norm_w, w_in_hgrn, hgrn_lower_bounds, hgrn_gnorm, w_in_attn, w_out, w_ffn_in, w_ffn_out):
    B, S, D = x.shape
    depth = norm_w.shape[0]
    T = B * S
    xf = x.reshape(T, D).astype(F32)
    lb = _lb_call(hgrn_lower_bounds).reshape(2 * depth, 1, D)
    tables = _rope_tables(S)

    hn = _norm_call(xf, w_pre=norm_w[0, 0])
    for layer in range(depth):
        slot = layer // 2
        if layer % 2 == 0:
            proj = _matmul(hn, w_in_hgrn, slot, tm=1024, tn=1024, out_dtype=F32)
            mixed = _hgrn_call(proj, lb, layer, depth, hgrn_gnorm[slot], B, S, D)
        else:
            proj = _matmul(hn, w_in_attn, slot, tm=1024, tn=1024, out_dtype=F32)
            mixed = _attn_call(proj, tables, B, S, D)
        h = _matmul(mixed, w_out, layer, tm=1024, tn=1024, out_dtype=BF16)
        xf, hn = _norm_call(xf, h, norm_w[layer, 1], norm_w[layer, 2])
        act = _matmul_swiglu(hn, w_ffn_in, layer, tm=1024, tn=512)
        h = _matmul(act, w_ffn_out, layer, tm=512, tn=512, out_dtype=BF16)
        if layer + 1 < depth:
            xf, hn = _norm_call(xf, h, norm_w[layer, 3], norm_w[layer + 1, 0])
        else:
            xf = _norm_call(xf, h, norm_w[layer, 3])
    return xf.reshape(B, S, D).astype(x.dtype)
```

```python
import functools
import math

import numpy as np
import jax
import jax.numpy as jnp
from jax import lax
from jax.experimental import pallas as pl
from jax.experimental.pallas import tpu as pltpu

F32 = jnp.float32
BF16 = jnp.bfloat16

EPS = 1e-6
HEAD = 128
DILATED_GROUPS = ((128, 1), (512, 4), (2048, 16))
ROPE_THETA = 500000.0
ROPE_DIM = HEAD // 4
NEG_INF = -1e30
LOG2E = math.log2(math.e)

GLA_CHUNK = 128
ATTN_BQ = 128
ATTN_RADIUS = 64
ATTN_KW = ATTN_BQ + 2 * ATTN_RADIUS
ATTN_P1 = 256
ATTN_UNROLL = 8

VMEM_LIMIT = 56 * 1024 * 1024

NT_DIMS = (((1,), (1,)), ((), ()))
TN_DIMS = (((0,), (0,)), ((), ()))


def _cparams(*sem):
    return pltpu.CompilerParams(dimension_semantics=sem, vmem_limit_bytes=VMEM_LIMIT)


def _rms(x, w):
    return x * lax.rsqrt(jnp.mean(x * x, axis=-1, keepdims=True) + EPS) * w


def _sigmoid(x):
    return 1.0 / (1.0 + jnp.exp(-x))


def _norm_kernel(*refs, has_h, has_pre):
    refs = list(refs)
    x_ref = refs.pop(0)
    x = x_ref[...]
    if has_h:
        h_ref = refs.pop(0)
        wpost_ref = refs.pop(0)
    if has_pre:
        wpre_ref = refs.pop(0)
    if has_h:
        xo_ref = refs.pop(0)
        x = x + _rms(h_ref[...].astype(F32), wpost_ref[...])
        xo_ref[...] = x
    if has_pre:
        hn_ref = refs.pop(0)
        hn_ref[...] = _rms(x, wpre_ref[...]).astype(BF16)


def _norm_call(x, h=None, w_post=None, w_pre=None, rows=256):
    T, D = x.shape
    has_h, has_pre = h is not None, w_pre is not None
    row_spec = pl.BlockSpec((rows, D), lambda i: (i, 0))
    w_spec = pl.BlockSpec((1, D), lambda i: (0, 0))
    args, in_specs, out_shape, out_specs = [x], [row_spec], [], []
    if has_h:
        args += [h, w_post.reshape(1, D)]
        in_specs += [row_spec, w_spec]
        out_shape.append(jax.ShapeDtypeStruct((T, D), F32))
        out_specs.append(row_spec)
    if has_pre:
        args.append(w_pre.reshape(1, D))
        in_specs.append(w_spec)
        out_shape.append(jax.ShapeDtypeStruct((T, D), BF16))
        out_specs.append(row_spec)
    outs = pl.pallas_call(
        functools.partial(_norm_kernel, has_h=has_h, has_pre=has_pre),
        grid=(T // rows,),
        in_specs=in_specs,
        out_specs=out_specs,
        out_shape=out_shape,
        compiler_params=_cparams("parallel"),
        name="norm_merge",
    )(*args)
    return outs if len(outs) > 1 else outs[0]


def _mm_kernel(x_ref, w_ref, o_ref, wb_ref):
    @pl.when(pl.program_id(1) == 0)
    def _():
        wb_ref[...] = w_ref[...].astype(BF16)

    o_ref[...] = jnp.dot(x_ref[...], wb_ref[...], preferred_element_type=F32).astype(o_ref.dtype)


def _matmul(x, w, layer, *, tm, tn, out_dtype):
    T, K = x.shape
    N = w.shape[2]
    return pl.pallas_call(
        _mm_kernel,
        grid=(N // tn, T // tm),
        in_specs=[
            pl.BlockSpec((tm, K), lambda j, i: (i, 0)),
            pl.BlockSpec((None, K, tn), lambda j, i: (layer, 0, j)),
        ],
        out_specs=pl.BlockSpec((tm, tn), lambda j, i: (i, j)),
        out_shape=jax.ShapeDtypeStruct((T, N), out_dtype),
        scratch_shapes=[pltpu.VMEM((K, tn), BF16)],
        compiler_params=_cparams("parallel", "arbitrary"),
        name="matmul",
    )(x, w)


def _swiglu_kernel(x_ref, wg_ref, wu_ref, o_ref, wgb_ref, wub_ref):
    @pl.when(pl.program_id(1) == 0)
    def _():
        wgb_ref[...] = wg_ref[...].astype(BF16)
        wub_ref[...] = wu_ref[...].astype(BF16)

    x = x_ref[...]
    gate = jnp.dot(x, wgb_ref[...], preferred_element_type=F32)
    up = jnp.dot(x, wub_ref[...], preferred_element_type=F32)
    o_ref[...] = (gate * _sigmoid(gate) * up).astype(o_ref.dtype)


def _matmul_swiglu(x, w, layer, *, tm, tn):
    T, K = x.shape
    F = w.shape[2] // 2
    nf = F // tn
    return pl.pallas_call(
        _swiglu_kernel,
        grid=(nf, T // tm),
        in_specs=[
            pl.BlockSpec((tm, K), lambda j, i: (i, 0)),
            pl.BlockSpec((None, K, tn), lambda j, i: (layer, 0, j)),
            pl.BlockSpec((None, K, tn), lambda j, i: (layer, 0, j + nf)),
        ],
        out_specs=pl.BlockSpec((tm, tn), lambda j, i: (i, j)),
        out_shape=jax.ShapeDtypeStruct((T, F), BF16),
        scratch_shapes=[pltpu.VMEM((K, tn), BF16), pltpu.VMEM((K, tn), BF16)],
        compiler_params=_cparams("parallel", "arbitrary"),
        name="matmul_swiglu",
    )(x, w, w)


def _outproj_kernel(a_ref, w_ref, x_ref, wpost_ref, wpre_ref, xo_ref, hn_ref, wb_ref):
    @pl.when(pl.program_id(0) == 0)
    def _():
        wb_ref[...] = w_ref[...].astype(BF16)

    h = jnp.dot(a_ref[...], wb_ref[...], preferred_element_type=F32)
    x = x_ref[...] + _rms(h, wpost_ref[...])
    xo_ref[...] = x
    hn_ref[...] = _rms(x, wpre_ref[...]).astype(BF16)


def _outproj_merge(a, w, layer, x, w_post, w_pre, *, tm):
    T, D = x.shape
    row = lambda dt: pl.BlockSpec((tm, D), lambda i: (i, 0))
    vec = pl.BlockSpec((1, D), lambda i: (0, 0))
    return pl.pallas_call(
        _outproj_kernel,
        grid=(T // tm,),
        in_specs=[row(BF16),
                  pl.BlockSpec((None, D, D), lambda i: (layer, 0, 0), pipeline_mode=pl.Buffered(1)),
                  row(F32), vec, vec],
        out_specs=[row(F32), row(BF16)],
        out_shape=[jax.ShapeDtypeStruct((T, D), F32), jax.ShapeDtypeStruct((T, D), BF16)],
        scratch_shapes=[pltpu.VMEM((D, D), BF16)],
        compiler_params=_cparams("arbitrary"),
        name="outproj_merge",
    )(a, w, x, w_post.reshape(1, D), w_pre.reshape(1, D))


def _lb_kernel(x_ref, o_ref, *, depth):
    for d in range(2):
        r = [x_ref[pl.ds(d * depth + i, 1), :] for i in range(depth)]
        m = functools.reduce(jnp.maximum, r)
        e = [jnp.exp(v - m) for v in r]
        tot = functools.reduce(lambda a, b: a + b, e)
        c = None
        first = None
        for i in range(depth):
            p = e[i] / tot
            c = p if c is None else c + p
            if first is None:
                first = c
            o_ref[pl.ds(d * depth + i, 1), :] = c - first


def _lb_call(lower_bounds):
    two, depth, D = lower_bounds.shape
    return pl.pallas_call(
        functools.partial(_lb_kernel, depth=depth),
        out_shape=jax.ShapeDtypeStruct((two * depth, D), F32),
        name="hgrn_lower_bounds",
    )(lower_bounds.reshape(two * depth, D).astype(F32))


def _gla_level_ids(C, rev):
    t = np.arange(C)[:, None]
    s = np.arange(C)[None, :]
    x = t ^ s
    lv = np.where(x > 0, np.floor(np.log2(np.maximum(x, 1))).astype(np.int64), int(math.log2(C)))
    allowed = (t <= s) if rev else (t >= s)
    return np.where(allowed, lv, -1).astype(np.int32)


def _gla_tri(C, rev):
    t = np.arange(C)[:, None]
    j = np.arange(C)[None, :]
    return ((j >= t) if rev else (j <= t)).astype(np.float32)


def _split_row_exponent(b, w, rev):
    C = b.shape[0]
    span = 2 * w
    mid = w if rev else w - 1
    pieces = []
    for a in range(C // span):
        lo, split, hi = a * span, a * span + w, (a + 1) * span
        row = b[lo + mid:lo + mid + 1, :]
        if w % 8 == 0:
            first, second = b[lo:split, :], b[split:hi, :]
            pieces += [first - row, row - second] if rev else [row - first, second - row]
        else:
            pieces.append(-jnp.abs(b[lo:hi, :] - row))
    return pieces[0] if len(pieces) == 1 else jnp.concatenate(pieces, axis=0)


def _fine_decay(fg, w, rev):
    C = fg.shape[0]
    t4 = lax.broadcasted_iota(jnp.int32, fg.shape, 0) & 3
    if w == 1:
        keep = (t4 & 1) == (0 if rev else 1)
        return jnp.where(keep, fg, 1.0)
    nxt = pltpu.roll(fg, C - 1, 0)
    prv = pltpu.roll(fg, 1, 0)
    by_row = [fg * nxt, fg, 1.0, prv] if rev else [nxt, 1.0, fg, fg * prv]
    return jnp.where(t4 == 0, by_row[0], jnp.where(t4 == 1, by_row[1], jnp.where(t4 == 2, by_row[2], by_row[3])))


def _gla_chunk(qb, f_raw, vb, lb, st_ref, lv_ref, tri_ref, rev):
    C = qb.shape[0]
    nlev = int(math.log2(C))
    fg = lb + (1.0 - lb) * _sigmoid(f_raw)
    kb = (1.0 - fg).astype(BF16)
    lg = jnp.log(fg) * LOG2E
    hi = lg.astype(BF16)
    lo = (lg - hi.astype(F32)).astype(BF16)
    tri = tri_ref[...]
    b = jnp.dot(tri, hi, preferred_element_type=F32) + jnp.dot(tri, lo, preferred_element_type=F32)

    lv = lv_ref[...]
    scores = jnp.where(lv == nlev, lax.dot_general(qb, kb, NT_DIMS, preferred_element_type=F32), 0.0)
    for p in range(nlev):
        w = 1 << p
        if w < 4:
            z = _fine_decay(fg, w, rev).astype(BF16)
        else:
            z = jnp.exp2(_split_row_exponent(b, w, rev)).astype(BF16)
        s = lax.dot_general(qb * z, kb * z, NT_DIMS, preferred_element_type=F32)
        scores = jnp.where(lv == p, s, scores)

    b_edge = b[0:1, :] if rev else b[C - 1:C, :]
    q_st = qb * jnp.exp2(b).astype(BF16)
    k_st = kb * jnp.exp2(b_edge - b).astype(BF16)
    st = st_ref[...]
    o = (jnp.dot(scores.astype(BF16), vb, preferred_element_type=F32)
         + lax.dot_general(q_st, st.astype(BF16), NT_DIMS, preferred_element_type=F32))
    st_ref[...] = st * jnp.exp2(b_edge) + lax.dot_general(vb, k_st, TN_DIMS, preferred_element_type=F32)
    return o


def _hgrn_kernel(q_ref, ff_ref, fb_ref, i_ref, g_ref, lbf_ref, lbb_ref, gn_ref,
                 lvf_ref, lvb_ref, trif_ref, trib_ref, o_ref,
                 qs_ref, vs_ref, of_ref, ob_ref, stf_ref, stb_ref, *, chunk):
    S = q_ref.shape[0]
    n_chunks = S // chunk

    def chunk_rows(c):
        return pl.ds(pl.multiple_of(c * chunk, chunk), chunk)

    def stage_body(c, carry):
        rows = chunk_rows(c)
        qr = q_ref[rows, :]
        qs_ref[rows, :] = (qr * _sigmoid(qr)).astype(BF16)
        vs_ref[rows, :] = i_ref[rows, :].astype(BF16)
        return carry

    lax.fori_loop(0, n_chunks, stage_body, 0)
    stf_ref[...] = jnp.zeros_like(stf_ref)
    stb_ref[...] = jnp.zeros_like(stb_ref)

    def scan_body(j, carry):
        rf = chunk_rows(j)
        rb = chunk_rows(n_chunks - 1 - j)
        of_ref[rf, :] = _gla_chunk(qs_ref[rf, :], ff_ref[rf, :], vs_ref[rf, :], lbf_ref[...],
                                   stf_ref, lvf_ref, trif_ref, False)
        ob_ref[rb, :] = _gla_chunk(qs_ref[rb, :], fb_ref[rb, :], vs_ref[rb, :], lbb_ref[...],
                                   stb_ref, lvb_ref, trib_ref, True)
        return carry

    lax.fori_loop(0, n_chunks, scan_body, 0, unroll=2)

    def out_body(c, carry):
        rows = chunk_rows(c)
        o = of_ref[rows, :] + ob_ref[rows, :]
        gate = g_ref[rows, :]
        o_ref[rows, :] = (_rms(o, gn_ref[...]) * (gate * _sigmoid(gate))).astype(o_ref.dtype)
        return carry

    lax.fori_loop(0, n_chunks, out_body, 0)


def _hgrn_call(proj, lb, layer, depth, gnorm, B, S, D):
    H = D // HEAD
    C = GLA_CHUNK
    col = lambda c: pl.BlockSpec((S, HEAD), lambda b, h, c=c: (b, c * H + h))
    lb_spec = lambda r: pl.BlockSpec((None, 1, HEAD), lambda b, h, r=r: (r, 0, h))
    const = lambda shape: pl.BlockSpec(shape, lambda b, h: (0, 0))
    return pl.pallas_call(
        functools.partial(_hgrn_kernel, chunk=C),
        grid=(B, H),
        in_specs=[col(0), col(1), col(2), col(3), col(4),
                  lb_spec(layer), lb_spec(depth + layer), const((1, HEAD)),
                  const((C, C)), const((C, C)), const((C, C)), const((C, C))],
        out_specs=pl.BlockSpec((S, HEAD), lambda b, h: (b, h)),
        out_shape=jax.ShapeDtypeStruct((B * S, D), BF16),
        scratch_shapes=[pltpu.VMEM((S, HEAD), BF16), pltpu.VMEM((S, HEAD), BF16),
                        pltpu.VMEM((S, HEAD), F32), pltpu.VMEM((S, HEAD), F32),
                        pltpu.VMEM((HEAD, HEAD), F32), pltpu.VMEM((HEAD, HEAD), F32)],
        compiler_params=_cparams("parallel", "parallel"),
        name="hgrn2_mixer",
    )(proj, proj, proj, proj, proj, lb, lb, gnorm.reshape(1, HEAD),
      jnp.asarray(_gla_level_ids(C, False)), jnp.asarray(_gla_level_ids(C, True)),
      jnp.asarray(_gla_tri(C, False), BF16), jnp.asarray(_gla_tri(C, True), BF16))


def _rope_tables(S):
    pos = jnp.arange(S, dtype=F32)
    inv_freq = ROPE_THETA ** (-(jnp.arange(0, ROPE_DIM, 2, dtype=F32) / ROPE_DIM))
    ang = pos[:, None] * inv_freq[None, :]
    ang = jnp.concatenate([ang, ang], axis=-1)
    pad = lambda t, val: jnp.concatenate([t, jnp.full((S, HEAD - t.shape[1]), val, F32)], axis=-1)
    cos_t, sin_t = pad(jnp.cos(ang), 1.0), pad(jnp.sin(ang), 0.0)
    qs = HEAD ** -0.5 * math.log2(math.e)
    per_group = []
    for _, d in DILATED_GROUPS:
        order = lambda t: t.reshape(S // d, d, HEAD).transpose(1, 0, 2).reshape(S, HEAD)
        per_group.append(jnp.stack([order(cos_t) * qs, order(sin_t) * qs, order(cos_t), order(sin_t)]))
    return jnp.stack(per_group)


def _rotate_half_matrix():
    half = ROPE_DIM // 2
    p = np.zeros((HEAD, HEAD), np.float32)
    for l in range(half):
        p[l + half, l] = -1.0
        p[l, l + half] = 1.0
    return p


def _attn_bias():
    i = np.arange(ATTN_BQ)[:, None]
    j = np.arange(ATTN_KW)[None, :]
    band = lambda off: np.abs(j - i - off) <= ATTN_RADIUS
    r = ATTN_RADIUS
    masks = [band(0), band(r), band(r) & (j >= r), band(r) & (j < r + ATTN_BQ), band(2 * r)]
    return np.where(np.stack(masks), 0.0, NEG_INF).astype(np.float32)


def _attn_group(q_ref, k_ref, v_ref, tab_ref, rot_ref, bias_ref, o_ref,
                qs_ref, ks_ref, vs_ref, m_ref, l_ref, acc_ref, *, d, first, last):
    S = q_ref.shape[0]
    L = S // d
    assert L >= ATTN_KW and L % ATTN_BQ == 0
    log_l = int(math.log2(L))

    def natural_rows(p0, n):
        if d == 1:
            return pl.ds(p0, n)
        return pl.ds((p0 >> log_l) + d * (p0 & (L - 1)), n, stride=d)

    def rope(x, cos, sin):
        rot = jnp.dot(x.astype(BF16), rot_ref[...], preferred_element_type=F32)
        return (x * cos + rot * sin).astype(BF16)

    def stage_body(i, carry):
        p0 = pl.multiple_of(i * ATTN_P1, ATTN_P1)
        src = natural_rows(p0, ATTN_P1)
        dst = pl.ds(p0, ATTN_P1)
        qs_ref[dst, :] = rope(q_ref[src, :], tab_ref[0, dst, :], tab_ref[1, dst, :])
        ks_ref[dst, :] = rope(k_ref[src, :], tab_ref[2, dst, :], tab_ref[3, dst, :])
        vs_ref[dst, :] = v_ref[src, :].astype(BF16)
        return carry

    lax.fori_loop(0, S // ATTN_P1, stage_body, 0, unroll=4)

    def one_block(blk):
        p0 = pl.multiple_of(blk * ATTN_BQ, ATTN_BQ)
        k0 = pl.multiple_of(jnp.clip(p0 - ATTN_RADIUS, 0, S - ATTN_KW), ATTN_RADIUS)
        at_start = ((p0 & (L - 1)) == 0).astype(jnp.int32)
        at_end = (((p0 + ATTN_BQ) & (L - 1)) == 0).astype(jnp.int32)
        geom = jnp.where(p0 == 0, 0, jnp.where(p0 == S - ATTN_BQ, 4, 1 + at_start + 2 * at_end))
        s = lax.dot_general(qs_ref[pl.ds(p0, ATTN_BQ), :], ks_ref[pl.ds(k0, ATTN_KW), :], NT_DIMS,
                            preferred_element_type=F32)
        s = s + bias_ref[geom]
        m = jnp.max(s, axis=-1, keepdims=True)
        p = jnp.exp2(s - m)
        l = jnp.sum(p, axis=-1, keepdims=True)
        acc = jnp.dot(p.astype(BF16), vs_ref[pl.ds(k0, ATTN_KW), :], preferred_element_type=F32)
        m = jnp.broadcast_to(m, acc.shape)
        l = jnp.broadcast_to(l, acc.shape)
        nat = natural_rows(p0, ATTN_BQ)
        if not first:
            m_old = m_ref[nat, :]
            m_new = jnp.maximum(m_old, m)
            a_old = jnp.exp2(m_old - m_new)
            a_new = jnp.exp2(m - m_new)
            l = a_old * l_ref[nat, :] + a_new * l
            acc = a_old * acc_ref[nat, :] + a_new * acc
            m = m_new
        if last:
            o_ref[nat, :] = (acc / l).astype(o_ref.dtype)
        else:
            m_ref[nat, :] = m
            l_ref[nat, :] = l
            acc_ref[nat, :] = acc

    unroll = ATTN_UNROLL * 2 if d == 1 else ATTN_UNROLL

    def block_body(i, carry):
        for u in range(unroll):
            one_block(i * unroll + u)
        return carry

    lax.fori_loop(0, S // (ATTN_BQ * unroll), block_body, 0)


def _attn_kernel(q_ref, k_ref, v_ref, tab_ref, rot_ref, bias_ref, o_ref,
                 qs_ref, ks_ref, vs_ref, m_ref, l_ref, acc_ref):
    g = pl.program_id(2)
    n_groups = len(DILATED_GROUPS)
    assert DILATED_GROUPS[0][1] == 1
    for step in range(n_groups):
        window, dilation = DILATED_GROUPS[n_groups - 1 - step]
        assert window // (2 * dilation) == ATTN_RADIUS

        @pl.when(g == step)
        def _(step=step, dilation=dilation):
            _attn_group(q_ref, k_ref, v_ref, tab_ref, rot_ref, bias_ref, o_ref,
                        qs_ref, ks_ref, vs_ref, m_ref, l_ref, acc_ref,
                        d=dilation, first=step == 0, last=step == n_groups - 1)


def _attn_call(proj, tables, B, S, D):
    H = D // HEAD
    n_groups = len(DILATED_GROUPS)
    col = lambda j: pl.BlockSpec((S, HEAD), lambda b, h, g, j=j: (b, ((n_groups - 1 - g) * 3 + j) * H + h))
    return pl.pallas_call(
        _attn_kernel,
        grid=(B, H, n_groups),
        in_specs=[col(0), col(1), col(2),
                  pl.BlockSpec((None, 4, S, HEAD), lambda b, h, g: (n_groups - 1 - g, 0, 0, 0)),
                  pl.BlockSpec((HEAD, HEAD), lambda b, h, g: (0, 0)),
                  pl.BlockSpec((5, ATTN_BQ, ATTN_KW), lambda b, h, g: (0, 0, 0))],
        out_specs=pl.BlockSpec((S, HEAD), lambda b, h, g: (b, h)),
        out_shape=jax.ShapeDtypeStruct((B * S, D), BF16),
        scratch_shapes=[pltpu.VMEM((S, HEAD), BF16), pltpu.VMEM((S, HEAD), BF16), pltpu.VMEM((S, HEAD), BF16),
                        pltpu.VMEM((S, HEAD), F32), pltpu.VMEM((S, HEAD), F32), pltpu.VMEM((S, HEAD), F32)],
        compiler_params=_cparams("parallel", "parallel", "arbitrary"),
        name="dilated_attention",
    )(proj, proj, proj, tables, jnp.asarray(_rotate_half_matrix(), BF16), jnp.asarray(_attn_bias()))


def kernel(x, norm_w, w_in_hgrn, hgrn_lower_bounds, hgrn_gnorm, w_in_attn, w_out, w_ffn_in, w_ffn_out):
    B, S, D = x.shape
    depth = norm_w.shape[0]
    T = B * S
    xf = x.reshape(T, D).astype(F32)
    lb = _lb_call(hgrn_lower_bounds).reshape(2 * depth, 1, D)
    tables = _rope_tables(S)

    hn = _norm_call(xf, w_pre=norm_w[0, 0])
    for layer in range(depth):
        slot = layer // 2
        if layer % 2 == 0:
            proj = _matmul(hn, w_in_hgrn, slot, tm=1024, tn=1024, out_dtype=F32)
            mixed = _hgrn_call(proj, lb, layer, depth, hgrn_gnorm[slot], B, S, D)
        else:
            proj = _matmul(hn, w_in_attn, slot, tm=1024, tn=1024, out_dtype=F32)
            mixed = _attn_call(proj, tables, B, S, D)
        xf, hn = _outproj_merge(mixed, w_out, layer, xf, norm_w[layer, 1], norm_w[layer, 2], tm=512)
        act = _matmul_swiglu(hn, w_ffn_in, layer, tm=1024, tn=512)
        h = _matmul(act, w_ffn_out, layer, tm=512, tn=512, out_dtype=BF16)
        if layer + 1 < depth:
            xf, hn = _norm_call(xf, h, norm_w[layer, 3], norm_w[layer + 1, 0])
        else:
            xf = _norm_call(xf, h, norm_w[layer, 3])
    return xf.reshape(B, S, D).astype(x.dtype)
```

```python
import functools
import math

import numpy as np
import jax
import jax.numpy as jnp
from jax import lax
from jax.experimental import pallas as pl
from jax.experimental.pallas import tpu as pltpu

F32 = jnp.float32
BF16 = jnp.bfloat16

EPS = 1e-6
HEAD = 128
DILATED_GROUPS = ((128, 1), (512, 4), (2048, 16))
ROPE_THETA = 500000.0
ROPE_DIM = HEAD // 4
NEG_INF = -1e30
LOG2E = math.log2(math.e)

GLA_CHUNK = 128
ATTN_BQ = 128
ATTN_RADIUS = 64
ATTN_KW = ATTN_BQ + 2 * ATTN_RADIUS
ATTN_P1 = 256
ATTN_UNROLL = 8

VMEM_LIMIT = 56 * 1024 * 1024

NT_DIMS = (((1,), (1,)), ((), ()))
TN_DIMS = (((0,), (0,)), ((), ()))


def _cparams(*sem):
    return pltpu.CompilerParams(dimension_semantics=sem, vmem_limit_bytes=VMEM_LIMIT)


def _rms(x, w):
    return x * lax.rsqrt(jnp.mean(x * x, axis=-1, keepdims=True) + EPS) * w


def _sigmoid(x):
    return 1.0 / (1.0 + jnp.exp(-x))


def _norm_kernel(*refs, has_h, has_pre):
    refs = list(refs)
    x_ref = refs.pop(0)
    x = x_ref[...]
    if has_h:
        h_ref = refs.pop(0)
        wpost_ref = refs.pop(0)
    if has_pre:
        wpre_ref = refs.pop(0)
    if has_h:
        xo_ref = refs.pop(0)
        x = x + _rms(h_ref[...].astype(F32), wpost_ref[...])
        xo_ref[...] = x
    if has_pre:
        hn_ref = refs.pop(0)
        hn_ref[...] = _rms(x, wpre_ref[...]).astype(BF16)


def _norm_call(x, h=None, w_post=None, w_pre=None, rows=256):
    T, D = x.shape
    has_h, has_pre = h is not None, w_pre is not None
    row_spec = pl.BlockSpec((rows, D), lambda i: (i, 0))
    w_spec = pl.BlockSpec((1, D), lambda i: (0, 0))
    args, in_specs, out_shape, out_specs = [x], [row_spec], [], []
    if has_h:
        args += [h, w_post.reshape(1, D)]
        in_specs += [row_spec, w_spec]
        out_shape.append(jax.ShapeDtypeStruct((T, D), F32))
        out_specs.append(row_spec)
    if has_pre:
        args.append(w_pre.reshape(1, D))
        in_specs.append(w_spec)
        out_shape.append(jax.ShapeDtypeStruct((T, D), BF16))
        out_specs.append(row_spec)
    outs = pl.pallas_call(
        functools.partial(_norm_kernel, has_h=has_h, has_pre=has_pre),
        grid=(T // rows,),
        in_specs=in_specs,
        out_specs=out_specs,
        out_shape=out_shape,
        compiler_params=_cparams("parallel"),
        name="norm_merge",
    )(*args)
    return outs if len(outs) > 1 else outs[0]


def _mm_kernel(x_ref, w_ref, o_ref, wb_ref):
    @pl.when(pl.program_id(1) == 0)
    def _():
        wb_ref[...] = w_ref[...].astype(BF16)

    o_ref[...] = jnp.dot(x_ref[...], wb_ref[...], preferred_element_type=F32).astype(o_ref.dtype)


def _matmul(x, w, layer, *, tm, tn, out_dtype):
    T, K = x.shape
    N = w.shape[2]
    return pl.pallas_call(
        _mm_kernel,
        grid=(N // tn, T // tm),
        in_specs=[
            pl.BlockSpec((tm, K), lambda j, i: (i, 0)),
            pl.BlockSpec((None, K, tn), lambda j, i: (layer, 0, j)),
        ],
        out_specs=pl.BlockSpec((tm, tn), lambda j, i: (i, j)),
        out_shape=jax.ShapeDtypeStruct((T, N), out_dtype),
        scratch_shapes=[pltpu.VMEM((K, tn), BF16)],
        compiler_params=_cparams("parallel", "arbitrary"),
        name="matmul",
    )(x, w)


def _swiglu_kernel(x_ref, wg_ref, wu_ref, o_ref, wgb_ref, wub_ref):
    @pl.when(pl.program_id(1) == 0)
    def _():
        wgb_ref[...] = wg_ref[...].astype(BF16)
        wub_ref[...] = wu_ref[...].astype(BF16)

    x = x_ref[...]
    gate = jnp.dot(x, wgb_ref[...], preferred_element_type=F32)
    up = jnp.dot(x, wub_ref[...], preferred_element_type=F32)
    o_ref[...] = (gate * _sigmoid(gate) * up).astype(o_ref.dtype)


def _matmul_swiglu(x, w, layer, *, tm, tn):
    T, K = x.shape
    F = w.shape[2] // 2
    nf = F // tn
    return pl.pallas_call(
        _swiglu_kernel,
        grid=(nf, T // tm),
        in_specs=[
            pl.BlockSpec((tm, K), lambda j, i: (i, 0)),
            pl.BlockSpec((None, K, tn), lambda j, i: (layer, 0, j)),
            pl.BlockSpec((None, K, tn), lambda j, i: (layer, 0, j + nf)),
        ],
        out_specs=pl.BlockSpec((tm, tn), lambda j, i: (i, j)),
        out_shape=jax.ShapeDtypeStruct((T, F), BF16),
        scratch_shapes=[pltpu.VMEM((K, tn), BF16), pltpu.VMEM((K, tn), BF16)],
        compiler_params=_cparams("parallel", "arbitrary"),
        name="matmul_swiglu",
    )(x, w, w)


def _outproj_kernel(a_ref, w_ref, x_ref, wpost_ref, wpre_ref, xo_ref, hn_ref, wb_ref):
    @pl.when(pl.program_id(0) == 0)
    def _():
        wb_ref[...] = w_ref[...].astype(BF16)

    h = jnp.dot(a_ref[...], wb_ref[...], preferred_element_type=F32)
    x = x_ref[...] + _rms(h, wpost_ref[...])
    xo_ref[...] = x
    hn_ref[...] = _rms(x, wpre_ref[...]).astype(BF16)


def _outproj_merge(a, w, layer, x, w_post, w_pre, *, tm):
    T, D = x.shape
    row = lambda dt: pl.BlockSpec((tm, D), lambda i: (i, 0))
    vec = pl.BlockSpec((1, D), lambda i: (0, 0))
    return pl.pallas_call(
        _outproj_kernel,
        grid=(T // tm,),
        in_specs=[row(BF16),
                  pl.BlockSpec((None, D, D), lambda i: (layer, 0, 0), pipeline_mode=pl.Buffered(1)),
                  row(F32), vec, vec],
        out_specs=[row(F32), row(BF16)],
        out_shape=[jax.ShapeDtypeStruct((T, D), F32), jax.ShapeDtypeStruct((T, D), BF16)],
        scratch_shapes=[pltpu.VMEM((D, D), BF16)],
        compiler_params=_cparams("arbitrary"),
        name="outproj_merge",
    )(a, w, x, w_post.reshape(1, D), w_pre.reshape(1, D))


def _lb_kernel(x_ref, o_ref, *, depth):
    for d in range(2):
        r = [x_ref[pl.ds(d * depth + i, 1), :] for i in range(depth)]
        m = functools.reduce(jnp.maximum, r)
        e = [jnp.exp(v - m) for v in r]
        tot = functools.reduce(lambda a, b: a + b, e)
        c = None
        first = None
        for i in range(depth):
            p = e[i] / tot
            c = p if c is None else c + p
            if first is None:
                first = c
            o_ref[pl.ds(d * depth + i, 1), :] = c - first


def _lb_call(lower_bounds):
    two, depth, D = lower_bounds.shape
    return pl.pallas_call(
        functools.partial(_lb_kernel, depth=depth),
        out_shape=jax.ShapeDtypeStruct((two * depth, D), F32),
        name="hgrn_lower_bounds",
    )(lower_bounds.reshape(two * depth, D).astype(F32))


def _gla_level_ids(C, rev):
    t = np.arange(C)[:, None]
    s = np.arange(C)[None, :]
    x = t ^ s
    lv = np.where(x > 0, np.floor(np.log2(np.maximum(x, 1))).astype(np.int64), int(math.log2(C)))
    allowed = (t <= s) if rev else (t >= s)
    return np.where(allowed, lv, -1).astype(np.int32)


def _gla_tri(C, rev):
    t = np.arange(C)[:, None]
    j = np.arange(C)[None, :]
    return ((j >= t) if rev else (j <= t)).astype(np.float32)


def _split_row_exponent(b, w, rev):
    C = b.shape[0]
    span = 2 * w
    mid = w if rev else w - 1
    pieces = []
    for a in range(C // span):
        lo, split, hi = a * span, a * span + w, (a + 1) * span
        row = b[lo + mid:lo + mid + 1, :]
        if w % 8 == 0:
            first, second = b[lo:split, :], b[split:hi, :]
            pieces += [first - row, row - second] if rev else [row - first, second - row]
        else:
            pieces.append(-jnp.abs(b[lo:hi, :] - row))
    return pieces[0] if len(pieces) == 1 else jnp.concatenate(pieces, axis=0)


def _fine_decay(fg, w, rev):
    C = fg.shape[0]
    t4 = lax.broadcasted_iota(jnp.int32, fg.shape, 0) & 3
    if w == 1:
        keep = (t4 & 1) == (0 if rev else 1)
        return jnp.where(keep, fg, 1.0)
    nxt = pltpu.roll(fg, C - 1, 0)
    prv = pltpu.roll(fg, 1, 0)
    by_row = [fg * nxt, fg, 1.0, prv] if rev else [nxt, 1.0, fg, fg * prv]
    return jnp.where(t4 == 0, by_row[0], jnp.where(t4 == 1, by_row[1], jnp.where(t4 == 2, by_row[2], by_row[3])))


def _gla_chunk(qb, f_raw, vb, lb, st_ref, lv_ref, tri_ref, rev):
    C = qb.shape[0]
    nlev = int(math.log2(C))
    fg = lb + (1.0 - lb) * _sigmoid(f_raw)
    kb = (1.0 - fg).astype(BF16)
    lg = jnp.log(fg) * LOG2E
    hi = lg.astype(BF16)
    lo = (lg - hi.astype(F32)).astype(BF16)
    tri = tri_ref[...]
    b = jnp.dot(tri, hi, preferred_element_type=F32) + jnp.dot(tri, lo, preferred_element_type=F32)

    lv = lv_ref[...]
    scores = jnp.where(lv == nlev, lax.dot_general(qb, kb, NT_DIMS, preferred_element_type=F32), 0.0)
    for p in range(nlev):
        w = 1 << p
        if w < 4:
            z = _fine_decay(fg, w, rev).astype(BF16)
        else:
            z = jnp.exp2(_split_row_exponent(b, w, rev)).astype(BF16)
        s = lax.dot_general(qb * z, kb * z, NT_DIMS, preferred_element_type=F32)
        scores = jnp.where(lv == p, s, scores)

    b_edge = b[0:1, :] if rev else b[C - 1:C, :]
    q_st = qb * jnp.exp2(b).astype(BF16)
    k_st = kb * jnp.exp2(b_edge - b).astype(BF16)
    st = st_ref[...]
    o = (jnp.dot(scores.astype(BF16), vb, preferred_element_type=F32)
         + lax.dot_general(q_st, st.astype(BF16), NT_DIMS, preferred_element_type=F32))
    st_ref[...] = st * jnp.exp2(b_edge) + lax.dot_general(vb, k_st, TN_DIMS, preferred_element_type=F32)
    return o


def _hgrn_kernel(q_ref, ff_ref, fb_ref, i_ref, g_ref, lbf_ref, lbb_ref, gn_ref,
                 lvf_ref, lvb_ref, trif_ref, trib_ref, o_ref,
                 qs_ref, vs_ref, of_ref, ob_ref, stf_ref, stb_ref, *, chunk):
    S = q_ref.shape[0]
    n_chunks = S // chunk

    def chunk_rows(c):
        return pl.ds(pl.multiple_of(c * chunk, chunk), chunk)

    def stage_body(c, carry):
        rows = chunk_rows(c)
        qr = q_ref[rows, :]
        qs_ref[rows, :] = (qr * _sigmoid(qr)).astype(BF16)
        vs_ref[rows, :] = i_ref[rows, :].astype(BF16)
        return carry

    lax.fori_loop(0, n_chunks, stage_body, 0, unroll=4)
    stf_ref[...] = jnp.zeros_like(stf_ref)
    stb_ref[...] = jnp.zeros_like(stb_ref)

    def scan_body(j, carry):
        rf = chunk_rows(j)
        rb = chunk_rows(n_chunks - 1 - j)
        of_ref[rf, :] = _gla_chunk(qs_ref[rf, :], ff_ref[rf, :], vs_ref[rf, :], lbf_ref[...],
                                   stf_ref, lvf_ref, trif_ref, False)
        ob_ref[rb, :] = _gla_chunk(qs_ref[rb, :], fb_ref[rb, :], vs_ref[rb, :], lbb_ref[...],
                                   stb_ref, lvb_ref, trib_ref, True)
        return carry

    lax.fori_loop(0, n_chunks, scan_body, 0, unroll=4)

    def out_body(c, carry):
        rows = chunk_rows(c)
        o = of_ref[rows, :] + ob_ref[rows, :]
        gate = g_ref[rows, :]
        o_ref[rows, :] = (_rms(o, gn_ref[...]) * (gate * _sigmoid(gate))).astype(o_ref.dtype)
        return carry

    lax.fori_loop(0, n_chunks, out_body, 0, unroll=4)


def _hgrn_call(proj, lb, layer, depth, gnorm, B, S, D):
    H = D // HEAD
    C = GLA_CHUNK
    col = lambda c: pl.BlockSpec((S, HEAD), lambda b, h, c=c: (b, c * H + h))
    lb_spec = lambda r: pl.BlockSpec((None, 1, HEAD), lambda b, h, r=r: (r, 0, h))
    const = lambda shape: pl.BlockSpec(shape, lambda b, h: (0, 0))
    return pl.pallas_call(
        functools.partial(_hgrn_kernel, chunk=C),
        grid=(B, H),
        in_specs=[col(0), col(1), col(2), col(3), col(4),
                  lb_spec(layer), lb_spec(depth + layer), const((1, HEAD)),
                  const((C, C)), const((C, C)), const((C, C)), const((C, C))],
        out_specs=pl.BlockSpec((S, HEAD), lambda b, h: (b, h)),
        out_shape=jax.ShapeDtypeStruct((B * S, D), BF16),
        scratch_shapes=[pltpu.VMEM((S, HEAD), BF16), pltpu.VMEM((S, HEAD), BF16),
                        pltpu.VMEM((S, HEAD), F32), pltpu.VMEM((S, HEAD), F32),
                        pltpu.VMEM((HEAD, HEAD), F32), pltpu.VMEM((HEAD, HEAD), F32)],
        compiler_params=_cparams("parallel", "parallel"),
        name="hgrn2_mixer",
    )(proj, proj, proj, proj, proj, lb, lb, gnorm.reshape(1, HEAD),
      jnp.asarray(_gla_level_ids(C, False)), jnp.asarray(_gla_level_ids(C, True)),
      jnp.asarray(_gla_tri(C, False), BF16), jnp.asarray(_gla_tri(C, True), BF16))


def _rope_tables(S):
    pos = jnp.arange(S, dtype=F32)
    inv_freq = ROPE_THETA ** (-(jnp.arange(0, ROPE_DIM, 2, dtype=F32) / ROPE_DIM))
    ang = pos[:, None] * inv_freq[None, :]
    ang = jnp.concatenate([ang, ang], axis=-1)
    pad = lambda t, val: jnp.concatenate([t, jnp.full((S, HEAD - t.shape[1]), val, F32)], axis=-1)
    cos_t, sin_t = pad(jnp.cos(ang), 1.0), pad(jnp.sin(ang), 0.0)
    qs = HEAD ** -0.5 * math.log2(math.e)
    per_group = []
    for _, d in DILATED_GROUPS:
        order = lambda t: t.reshape(S // d, d, HEAD).transpose(1, 0, 2).reshape(S, HEAD)
        per_group.append(jnp.stack([order(cos_t) * qs, order(sin_t) * qs, order(cos_t), order(sin_t)]))
    return jnp.stack(per_group)


def _rotate_half_matrix():
    half = ROPE_DIM // 2
    p = np.zeros((HEAD, HEAD), np.float32)
    for l in range(half):
        p[l + half, l] = -1.0
        p[l, l + half] = 1.0
    return p


def _attn_bias():
    i = np.arange(ATTN_BQ)[:, None]
    j = np.arange(ATTN_KW)[None, :]
    band = lambda off: np.abs(j - i - off) <= ATTN_RADIUS
    r = ATTN_RADIUS
    masks = [band(0), band(r), band(r) & (j >= r), band(r) & (j < r + ATTN_BQ), band(2 * r)]
    return np.where(np.stack(masks), 0.0, NEG_INF).astype(np.float32)


def _attn_group(q_ref, k_ref, v_ref, tab_ref, rot_ref, bias_ref, o_ref,
                qs_ref, ks_ref, vs_ref, m_ref, l_ref, acc_ref, *, d, first, last):
    S = q_ref.shape[0]
    L = S // d
    assert L >= ATTN_KW and L % ATTN_BQ == 0
    log_l = int(math.log2(L))

    def natural_rows(p0, n):
        if d == 1:
            return pl.ds(p0, n)
        return pl.ds((p0 >> log_l) + d * (p0 & (L - 1)), n, stride=d)

    def rope(x, cos, sin):
        rot = jnp.dot(x.astype(BF16), rot_ref[...], preferred_element_type=F32)
        return (x * cos + rot * sin).astype(BF16)

    def stage_body(i, carry):
        p0 = pl.multiple_of(i * ATTN_P1, ATTN_P1)
        src = natural_rows(p0, ATTN_P1)
        dst = pl.ds(p0, ATTN_P1)
        qs_ref[dst, :] = rope(q_ref[src, :], tab_ref[0, dst, :], tab_ref[1, dst, :])
        ks_ref[dst, :] = rope(k_ref[src, :], tab_ref[2, dst, :], tab_ref[3, dst, :])
        vs_ref[dst, :] = v_ref[src, :].astype(BF16)
        return carry

    lax.fori_loop(0, S // ATTN_P1, stage_body, 0, unroll=4)

    def one_block(blk):
        p0 = pl.multiple_of(blk * ATTN_BQ, ATTN_BQ)
        k0 = pl.multiple_of(jnp.clip(p0 - ATTN_RADIUS, 0, S - ATTN_KW), ATTN_RADIUS)
        at_start = ((p0 & (L - 1)) == 0).astype(jnp.int32)
        at_end = (((p0 + ATTN_BQ) & (L - 1)) == 0).astype(jnp.int32)
        geom = jnp.where(p0 == 0, 0, jnp.where(p0 == S - ATTN_BQ, 4, 1 + at_start + 2 * at_end))
        s = lax.dot_general(qs_ref[pl.ds(p0, ATTN_BQ), :], ks_ref[pl.ds(k0, ATTN_KW), :], NT_DIMS,
                            preferred_element_type=F32)
        s = s + bias_ref[geom]
        m = jnp.max(s, axis=-1, keepdims=True)
        p = jnp.exp2(s - m)
        l = jnp.sum(p, axis=-1, keepdims=True)
        acc = jnp.dot(p.astype(BF16), vs_ref[pl.ds(k0, ATTN_KW), :], preferred_element_type=F32)
        m = jnp.broadcast_to(m, acc.shape)
        l = jnp.broadcast_to(l, acc.shape)
        nat = natural_rows(p0, ATTN_BQ)
        if not first:
            m_old = m_ref[nat, :]
            m_new = jnp.maximum(m_old, m)
            a_old = jnp.exp2(m_old - m_new)
            a_new = jnp.exp2(m - m_new)
            l = a_old * l_ref[nat, :] + a_new * l
            acc = a_old * acc_ref[nat, :] + a_new * acc
            m = m_new
        if last:
            o_ref[nat, :] = (acc / l).astype(o_ref.dtype)
        else:
            m_ref[nat, :] = m
            l_ref[nat, :] = l
            acc_ref[nat, :] = acc

    unroll = ATTN_UNROLL * 2 if d == 1 else ATTN_UNROLL

    def block_body(i, carry):
        for u in range(unroll):
            one_block(i * unroll + u)
        return carry

    lax.fori_loop(0, S // (ATTN_BQ * unroll), block_body, 0)


def _attn_kernel(q_ref, k_ref, v_ref, tab_ref, rot_ref, bias_ref, o_ref,
                 qs_ref, ks_ref, vs_ref, m_ref, l_ref, acc_ref):
    g = pl.program_id(2)
    n_groups = len(DILATED_GROUPS)
    assert DILATED_GROUPS[0][1] == 1
    for step in range(n_groups):
        window, dilation = DILATED_GROUPS[n_groups - 1 - step]
        assert window // (2 * dilation) == ATTN_RADIUS

        @pl.when(g == step)
        def _(step=step, dilation=dilation):
            _attn_group(q_ref, k_ref, v_ref, tab_ref, rot_ref, bias_ref, o_ref,
                        qs_ref, ks_ref, vs_ref, m_ref, l_ref, acc_ref,
                        d=dilation, first=step == 0, last=step == n_groups - 1)


def _attn_call(proj, tables, B, S, D):
    H = D // HEAD
    n_groups = len(DILATED_GROUPS)
    col = lambda j: pl.BlockSpec((S, HEAD), lambda b, h, g, j=j: (b, ((n_groups - 1 - g) * 3 + j) * H + h))
    return pl.pallas_call(
        _attn_kernel,
        grid=(B, H, n_groups),
        in_specs=[col(0), col(1), col(2),
                  pl.BlockSpec((None, 4, S, HEAD), lambda b, h, g: (n_groups - 1 - g, 0, 0, 0)),
                  pl.BlockSpec((HEAD, HEAD), lambda b, h, g: (0, 0)),
                  pl.BlockSpec((5, ATTN_BQ, ATTN_KW), lambda b, h, g: (0, 0, 0))],
        out_specs=pl.BlockSpec((S, HEAD), lambda b, h, g: (b, h)),
        out_shape=jax.ShapeDtypeStruct((B * S, D), BF16),
        scratch_shapes=[pltpu.VMEM((S, HEAD), BF16), pltpu.VMEM((S, HEAD), BF16), pltpu.VMEM((S, HEAD), BF16),
                        pltpu.VMEM((S, HEAD), F32), pltpu.VMEM((S, HEAD), F32), pltpu.VMEM((S, HEAD), F32)],
        compiler_params=_cparams("parallel", "parallel", "arbitrary"),
        name="dilated_attention",
    )(proj, proj, proj, tables, jnp.asarray(_rotate_half_matrix(), BF16), jnp.asarray(_attn_bias()))


def kernel(x, norm_w, w_in_hgrn, hgrn_lower_bounds, hgrn_gnorm, w_in_attn, w_out, w_ffn_in, w_ffn_out):
    B, S, D = x.shape
    depth = norm_w.shape[0]
    T = B * S
    xf = x.reshape(T, D).astype(F32)
    lb = _lb_call(hgrn_lower_bounds).reshape(2 * depth, 1, D)
    tables = _rope_tables(S)

    hn = _norm_call(xf, w_pre=norm_w[0, 0])
    for layer in range(depth):
        slot = layer // 2
        if layer % 2 == 0:
            proj = _matmul(hn, w_in_hgrn, slot, tm=1024, tn=1024, out_dtype=F32)
            mixed = _hgrn_call(proj, lb, layer, depth, hgrn_gnorm[slot], B, S, D)
        else:
            proj = _matmul(hn, w_in_attn, slot, tm=1024, tn=1024, out_dtype=F32)
            mixed = _attn_call(proj, tables, B, S, D)
        xf, hn = _outproj_merge(mixed, w_out, layer, xf, norm_w[layer, 1], norm_w[layer, 2], tm=512)
        act = _matmul_swiglu(hn, w_ffn_in, layer, tm=1024, tn=512)
        h = _matmul(act, w_ffn_out, layer, tm=512, tn=512, out_dtype=BF16)
        if layer + 1 < depth:
            xf, hn = _norm_call(xf, h, norm_w[layer, 3], norm_w[layer + 1, 0])
        else:
            xf = _norm_call(xf, h, norm_w[layer, 3])
    return xf.reshape(B, S, D).astype(x.dtype)
```

```python
import functools
import math

import numpy as np
import jax
import jax.numpy as jnp
from jax import lax
from jax.experimental import pallas as pl
from jax.experimental.pallas import tpu as pltpu

F32 = jnp.float32
BF16 = jnp.bfloat16

EPS = 1e-6
HEAD = 128
DILATED_GROUPS = ((128, 1), (512, 4), (2048, 16))
ROPE_THETA = 500000.0
ROPE_DIM = HEAD // 4
NEG_INF = -1e30
LOG2E = math.log2(math.e)

GLA_CHUNK = 128
ATTN_BQ = 128
ATTN_RADIUS = 64
ATTN_KW = ATTN_BQ + 2 * ATTN_RADIUS
ATTN_UNROLL = 8

VMEM_LIMIT = 56 * 1024 * 1024
MXU_N = 256

NT_DIMS = (((1,), (1,)), ((), ()))
TN_DIMS = (((0,), (0,)), ((), ()))


def _cparams(*sem):
    return pltpu.CompilerParams(dimension_semantics=sem, vmem_limit_bytes=VMEM_LIMIT)


def _rms(x, w):
    return x * lax.rsqrt(jnp.mean(x * x, axis=-1, keepdims=True) + EPS) * w


def _sigmoid(x):
    return 1.0 / (1.0 + jnp.exp(-x))


def _norm_kernel(*refs, has_h, has_pre, dilations):
    refs = list(refs)
    x_ref = refs.pop(0)
    x = x_ref[...]
    if has_h:
        h_ref = refs.pop(0)
        wpost_ref = refs.pop(0)
    if has_pre:
        wpre_ref = refs.pop(0)
    if has_h:
        xo_ref = refs.pop(0)
        x = x + _rms(h_ref[...].astype(F32), wpost_ref[...])
        xo_ref[...] = x
    if has_pre:
        hn_ref = refs.pop(0)
        hn = _rms(x, wpre_ref[...])
        hn_ref[...] = hn.astype(BF16)
        if dilations:
            slab_ref = refs[-1]
            rows = hn.shape[0]
            n_slab = hn.shape[1] // HEAD
            for c in range(n_slab):
                slab_ref[c] = hn[:, c * HEAD:(c + 1) * HEAD]
            for d, hp_ref in zip(dilations, refs[:len(dilations)]):
                for c in range(n_slab):
                    for r in range(d):
                        hp_ref[r, :, c * HEAD:(c + 1) * HEAD] = (
                            slab_ref[c, pl.ds(r, rows // d, stride=d), :].astype(BF16))


def _norm_call(x, h=None, w_post=None, w_pre=None, rows=256, dilations=(), seq=None):
    T, D = x.shape
    has_h, has_pre = h is not None, w_pre is not None
    row_spec = pl.BlockSpec((rows, D), lambda i: (i, 0))
    w_spec = pl.BlockSpec((1, D), lambda i: (0, 0))
    args, in_specs, out_shape, out_specs, scratch = [x], [row_spec], [], [], []
    if has_h:
        args += [h, w_post.reshape(1, D)]
        in_specs += [row_spec, w_spec]
        out_shape.append(jax.ShapeDtypeStruct((T, D), F32))
        out_specs.append(row_spec)
    if has_pre:
        args.append(w_pre.reshape(1, D))
        in_specs.append(w_spec)
        out_shape.append(jax.ShapeDtypeStruct((T, D), BF16))
        out_specs.append(row_spec)
    if dilations:
        blocks_per_seq = seq // rows
        for d in dilations:
            out_shape.append(jax.ShapeDtypeStruct((T // seq, d, seq // d, D), BF16))
            out_specs.append(pl.BlockSpec((None, d, rows // d, D),
                                          lambda i: (i // blocks_per_seq, 0, i % blocks_per_seq, 0)))
        scratch.append(pltpu.VMEM((D // HEAD, rows, HEAD), F32))
    outs = pl.pallas_call(
        functools.partial(_norm_kernel, has_h=has_h, has_pre=has_pre, dilations=tuple(dilations)),
        grid=(T // rows,),
        in_specs=in_specs,
        out_specs=out_specs,
        out_shape=out_shape,
        scratch_shapes=scratch,
        compiler_params=_cparams("parallel"),
        name="norm_merge",
    )(*args)
    outs = [o.reshape(T, D) for o in outs]
    return outs if len(outs) > 1 else outs[0]


def _mm_kernel(x_ref, w_ref, o_ref, wb_ref):
    @pl.when(pl.program_id(1) == 0)
    def _():
        wb_ref[...] = w_ref[...].astype(BF16)

    o_ref[...] = jnp.dot(x_ref[...], wb_ref[...], preferred_element_type=F32).astype(o_ref.dtype)


def _matmul(x, w, layer, *, tm, tn, out_dtype):
    T, K = x.shape
    N = w.shape[2]
    return pl.pallas_call(
        _mm_kernel,
        grid=(N // tn, T // tm),
        in_specs=[
            pl.BlockSpec((tm, K), lambda j, i: (i, 0)),
            pl.BlockSpec((None, K, tn), lambda j, i: (layer, 0, j)),
        ],
        out_specs=pl.BlockSpec((tm, tn), lambda j, i: (i, j)),
        out_shape=jax.ShapeDtypeStruct((T, N), out_dtype),
        scratch_shapes=[pltpu.VMEM((K, tn), BF16)],
        compiler_params=_cparams("parallel", "arbitrary"),
        name="matmul",
    )(x, w)


def _swiglu_kernel(x_ref, wg_ref, wu_ref, o_ref, wgb_ref, wub_ref):
    @pl.when(pl.program_id(1) == 0)
    def _():
        wgb_ref[...] = wg_ref[...].astype(BF16)
        wub_ref[...] = wu_ref[...].astype(BF16)

    x = x_ref[...]
    gate = jnp.dot(x, wgb_ref[...], preferred_element_type=F32)
    up = jnp.dot(x, wub_ref[...], preferred_element_type=F32)
    o_ref[...] = (gate * _sigmoid(gate) * up).astype(o_ref.dtype)


def _matmul_swiglu(x, w, layer, *, tm, tn):
    T, K = x.shape
    F = w.shape[2] // 2
    nf = F // tn
    return pl.pallas_call(
        _swiglu_kernel,
        grid=(nf, T // tm),
        in_specs=[
            pl.BlockSpec((tm, K), lambda j, i: (i, 0)),
            pl.BlockSpec((None, K, tn), lambda j, i: (layer, 0, j)),
            pl.BlockSpec((None, K, tn), lambda j, i: (layer, 0, j + nf)),
        ],
        out_specs=pl.BlockSpec((tm, tn), lambda j, i: (i, j)),
        out_shape=jax.ShapeDtypeStruct((T, F), BF16),
        scratch_shapes=[pltpu.VMEM((K, tn), BF16), pltpu.VMEM((K, tn), BF16)],
        compiler_params=_cparams("parallel", "arbitrary"),
        name="matmul_swiglu",
    )(x, w, w)


def _outproj_kernel(a_ref, w_ref, x_ref, wpost_ref, wpre_ref, xo_ref, hn_ref, wb_ref):
    @pl.when(pl.program_id(0) == 0)
    def _():
        wb_ref[...] = w_ref[...].astype(BF16)

    h = jnp.dot(a_ref[...], wb_ref[...], preferred_element_type=F32)
    x = x_ref[...] + _rms(h, wpost_ref[...])
    xo_ref[...] = x
    hn_ref[...] = _rms(x, wpre_ref[...]).astype(BF16)


def _outproj_merge(a, w, layer, x, w_post, w_pre, *, tm):
    T, D = x.shape
    row = lambda dt: pl.BlockSpec((tm, D), lambda i: (i, 0))
    vec = pl.BlockSpec((1, D), lambda i: (0, 0))
    return pl.pallas_call(
        _outproj_kernel,
        grid=(T // tm,),
        in_specs=[row(BF16),
                  pl.BlockSpec((None, D, D), lambda i: (layer, 0, 0), pipeline_mode=pl.Buffered(1)),
                  row(F32), vec, vec],
        out_specs=[row(F32), row(BF16)],
        out_shape=[jax.ShapeDtypeStruct((T, D), F32), jax.ShapeDtypeStruct((T, D), BF16)],
        scratch_shapes=[pltpu.VMEM((D, D), BF16)],
        compiler_params=_cparams("arbitrary"),
        name="outproj_merge",
    )(a, w, x, w_post.reshape(1, D), w_pre.reshape(1, D))


def _lb_kernel(x_ref, o_ref, *, depth):
    for d in range(2):
        r = [x_ref[pl.ds(d * depth + i, 1), :] for i in range(depth)]
        m = functools.reduce(jnp.maximum, r)
        e = [jnp.exp(v - m) for v in r]
        tot = functools.reduce(lambda a, b: a + b, e)
        c = None
        first = None
        for i in range(depth):
            p = e[i] / tot
            c = p if c is None else c + p
            if first is None:
                first = c
            o_ref[pl.ds(d * depth + i, 1), :] = c - first


def _lb_call(lower_bounds):
    two, depth, D = lower_bounds.shape
    return pl.pallas_call(
        functools.partial(_lb_kernel, depth=depth),
        out_shape=jax.ShapeDtypeStruct((two * depth, D), F32),
        name="hgrn_lower_bounds",
    )(lower_bounds.reshape(two * depth, D).astype(F32))


def _gla_level_ids(C, rev):
    t = np.arange(C)[:, None]
    s = np.arange(C)[None, :]
    x = t ^ s
    lv = np.where(x > 0, np.floor(np.log2(np.maximum(x, 1))).astype(np.int64), int(math.log2(C)))
    allowed = (t <= s) if rev else (t >= s)
    return np.where(allowed, lv, -1).astype(np.int32)


def _gla_tri(C, rev):
    t = np.arange(C)[:, None]
    j = np.arange(C)[None, :]
    return ((j >= t) if rev else (j <= t)).astype(np.float32)


def _split_row_exponent(b, w, rev):
    C = b.shape[0]
    span = 2 * w
    mid = w if rev else w - 1
    pieces = []
    for a in range(C // span):
        lo, split, hi = a * span, a * span + w, (a + 1) * span
        row = b[lo + mid:lo + mid + 1, :]
        if w % 8 == 0:
            first, second = b[lo:split, :], b[split:hi, :]
            pieces += [first - row, row - second] if rev else [row - first, second - row]
        else:
            pieces.append(-jnp.abs(b[lo:hi, :] - row))
    return pieces[0] if len(pieces) == 1 else jnp.concatenate(pieces, axis=0)


def _fine_decay(fg, w, rev):
    C = fg.shape[0]
    t4 = lax.broadcasted_iota(jnp.int32, fg.shape, 0) & 3
    if w == 1:
        keep = (t4 & 1) == (0 if rev else 1)
        return jnp.where(keep, fg, 1.0)
    nxt = pltpu.roll(fg, C - 1, 0)
    prv = pltpu.roll(fg, 1, 0)
    by_row = [fg * nxt, fg, 1.0, prv] if rev else [nxt, 1.0, fg, fg * prv]
    return jnp.where(t4 == 0, by_row[0], jnp.where(t4 == 1, by_row[1], jnp.where(t4 == 2, by_row[2], by_row[3])))


def _gla_chunk(qb, f_raw, vb, lb, st_ref, lv_ref, tri_ref, rev):
    C = qb.shape[0]
    nlev = int(math.log2(C))
    fg = lb + (1.0 - lb) * _sigmoid(f_raw)
    kb = (1.0 - fg).astype(BF16)
    lg = jnp.log(fg) * LOG2E
    hi = lg.astype(BF16)
    lo = (lg - hi.astype(F32)).astype(BF16)
    tri = tri_ref[...]
    b = jnp.dot(tri, hi, preferred_element_type=F32) + jnp.dot(tri, lo, preferred_element_type=F32)

    lv = lv_ref[...]
    scores = jnp.where(lv == nlev, lax.dot_general(qb, kb, NT_DIMS, preferred_element_type=F32), 0.0)
    for p in range(nlev):
        w = 1 << p
        if w < 4:
            z = _fine_decay(fg, w, rev).astype(BF16)
        else:
            z = jnp.exp2(_split_row_exponent(b, w, rev)).astype(BF16)
        s = lax.dot_general(qb * z, kb * z, NT_DIMS, preferred_element_type=F32)
        scores = jnp.where(lv == p, s, scores)

    b_edge = b[0:1, :] if rev else b[C - 1:C, :]
    q_st = qb * jnp.exp2(b).astype(BF16)
    k_st = kb * jnp.exp2(b_edge - b).astype(BF16)
    st = st_ref[...]
    o = (jnp.dot(scores.astype(BF16), vb, preferred_element_type=F32)
         + lax.dot_general(q_st, st.astype(BF16), NT_DIMS, preferred_element_type=F32))
    st_ref[...] = st * jnp.exp2(b_edge) + lax.dot_general(vb, k_st, TN_DIMS, preferred_element_type=F32)
    return o


def _hgrn_kernel(q_ref, ff_ref, fb_ref, i_ref, g_ref, lbf_ref, lbb_ref, gn_ref,
                 lvf_ref, lvb_ref, trif_ref, trib_ref, o_ref,
                 qs_ref, vs_ref, of_ref, ob_ref, stf_ref, stb_ref, *, chunk):
    S = q_ref.shape[0]
    n_chunks = S // chunk

    def chunk_rows(c):
        return pl.ds(pl.multiple_of(c * chunk, chunk), chunk)

    def stage_body(c, carry):
        rows = chunk_rows(c)
        qr = q_ref[rows, :]
        qs_ref[rows, :] = (qr * _sigmoid(qr)).astype(BF16)
        vs_ref[rows, :] = i_ref[rows, :].astype(BF16)
        return carry

    lax.fori_loop(0, n_chunks, stage_body, 0, unroll=4)
    stf_ref[...] = jnp.zeros_like(stf_ref)
    stb_ref[...] = jnp.zeros_like(stb_ref)

    def scan_body(j, carry):
        rf = chunk_rows(j)
        rb = chunk_rows(n_chunks - 1 - j)
        of_ref[rf, :] = _gla_chunk(qs_ref[rf, :], ff_ref[rf, :], vs_ref[rf, :], lbf_ref[...],
                                   stf_ref, lvf_ref, trif_ref, False)
        ob_ref[rb, :] = _gla_chunk(qs_ref[rb, :], fb_ref[rb, :], vs_ref[rb, :], lbb_ref[...],
                                   stb_ref, lvb_ref, trib_ref, True)
        return carry

    lax.fori_loop(0, n_chunks, scan_body, 0, unroll=4)

    def out_body(c, carry):
        rows = chunk_rows(c)
        o = of_ref[rows, :] + ob_ref[rows, :]
        gate = g_ref[rows, :]
        o_ref[rows, :] = (_rms(o, gn_ref[...]) * (gate * _sigmoid(gate))).astype(o_ref.dtype)
        return carry

    lax.fori_loop(0, n_chunks, out_body, 0, unroll=4)


def _hgrn_call(proj, lb, layer, depth, gnorm, B, S, D):
    H = D // HEAD
    C = GLA_CHUNK
    col = lambda c: pl.BlockSpec((S, HEAD), lambda b, h, c=c: (b, c * H + h))
    lb_spec = lambda r: pl.BlockSpec((None, 1, HEAD), lambda b, h, r=r: (r, 0, h))
    const = lambda shape: pl.BlockSpec(shape, lambda b, h: (0, 0))
    return pl.pallas_call(
        functools.partial(_hgrn_kernel, chunk=C),
        grid=(B, H),
        in_specs=[col(0), col(1), col(2), col(3), col(4),
                  lb_spec(layer), lb_spec(depth + layer), const((1, HEAD)),
                  const((C, C)), const((C, C)), const((C, C)), const((C, C))],
        out_specs=pl.BlockSpec((S, HEAD), lambda b, h: (b, h)),
        out_shape=jax.ShapeDtypeStruct((B * S, D), BF16),
        scratch_shapes=[pltpu.VMEM((S, HEAD), BF16), pltpu.VMEM((S, HEAD), BF16),
                        pltpu.VMEM((S, HEAD), F32), pltpu.VMEM((S, HEAD), F32),
                        pltpu.VMEM((HEAD, HEAD), F32), pltpu.VMEM((HEAD, HEAD), F32)],
        compiler_params=_cparams("parallel", "parallel"),
        name="hgrn2_mixer",
    )(proj, proj, proj, proj, proj, lb, lb, gnorm.reshape(1, HEAD),
      jnp.asarray(_gla_level_ids(C, False)), jnp.asarray(_gla_level_ids(C, True)),
      jnp.asarray(_gla_tri(C, False), BF16), jnp.asarray(_gla_tri(C, True), BF16))


def _rope_tables(S):
    half = ROPE_DIM // 2
    pos = jnp.arange(S, dtype=F32)
    inv_freq = ROPE_THETA ** (-(jnp.arange(0, ROPE_DIM, 2, dtype=F32) / ROPE_DIM))
    ang = pos[:, None] * inv_freq[None, :]
    ang = jnp.concatenate([ang, ang], axis=-1)
    cos, sin = jnp.cos(ang), jnp.sin(ang)
    pad = lambda t, val: jnp.concatenate([t, jnp.full((S, HEAD - t.shape[1]), val, F32)], axis=-1)
    zeros = jnp.zeros((S, half), F32)
    k_tabs = [pad(cos, 1.0),
              pad(jnp.concatenate([-sin[:, :half], zeros], axis=-1), 0.0),
              pad(jnp.concatenate([zeros, sin[:, half:]], axis=-1), 0.0)]
    qs = HEAD ** -0.5 * LOG2E
    tabs = jnp.stack([t * qs for t in k_tabs] + k_tabs)
    return jnp.stack([tabs.reshape(6, S // d, d, HEAD).transpose(0, 2, 1, 3).reshape(6, S, HEAD)
                      for _, d in DILATED_GROUPS])


def _qkv_kernel(x_ref, w_ref, tab_ref, o_ref, wb_ref, *, rope):
    @pl.when(pl.program_id(1) == 0)
    def _():
        wb_ref[...] = w_ref[...].astype(BF16)

    tn = wb_ref.shape[1]
    half = ROPE_DIM // 2
    x = x_ref[...]
    for nt in range(tn // MXU_N):
        res = jnp.dot(x, wb_ref[:, nt * MXU_N:(nt + 1) * MXU_N], preferred_element_type=F32)
        for s in range(MXU_N // HEAD):
            y = res[:, s * HEAD:(s + 1) * HEAD]
            if rope:
                y = (y * tab_ref[0] + pltpu.roll(y, HEAD - half, 1) * tab_ref[1]
                     + pltpu.roll(y, half, 1) * tab_ref[2])
            lane0 = nt * MXU_N + s * HEAD
            o_ref[:, lane0:lane0 + HEAD] = y.astype(BF16)


def _qkv_call(x, w, layer, col0, tables, S, *, tm, tn):
    T, K = x.shape
    rope = tables is not None
    if not rope:
        tables = jnp.zeros((3, tm, HEAD), F32)
    blocks_per_seq = S // tm
    return pl.pallas_call(
        functools.partial(_qkv_kernel, rope=rope),
        grid=(K // tn, T // tm),
        in_specs=[
            pl.BlockSpec((tm, K), lambda j, i: (i, 0)),
            pl.BlockSpec((None, K, tn), lambda j, i: (layer, 0, col0 // tn + j)),
            pl.BlockSpec((3, tm, HEAD), (lambda j, i: (0, i % blocks_per_seq, 0)) if rope else (lambda j, i: (0, 0, 0))),
        ],
        out_specs=pl.BlockSpec((tm, tn), lambda j, i: (i, j)),
        out_shape=jax.ShapeDtypeStruct((T, K), BF16),
        scratch_shapes=[pltpu.VMEM((K, tn), BF16)],
        compiler_params=_cparams("parallel", "arbitrary"),
        name="qkv_proj",
    )(x, w, tables)


def _attn_bias():
    i = np.arange(ATTN_BQ)[:, None]
    j = np.arange(ATTN_KW)[None, :]
    band = lambda off: np.abs(j - i - off) <= ATTN_RADIUS
    r = ATTN_RADIUS
    masks = [band(0), band(r), band(r) & (j >= r), band(r) & (j < r + ATTN_BQ), band(2 * r)]
    return np.where(np.stack(masks), 0.0, NEG_INF).astype(np.float32)


def _attn_group(q_ref, k_ref, v_ref, bias_ref, o_ref, m_ref, l_ref, acc_ref, *, d, first, last):
    S = q_ref.shape[0]
    L = S // d
    assert L >= ATTN_KW and L % ATTN_BQ == 0
    log_l = int(math.log2(L))

    def natural_rows(p0, n):
        if d == 1:
            return pl.ds(p0, n)
        return pl.ds((p0 >> log_l) + d * (p0 & (L - 1)), n, stride=d)

    def one_block(blk):
        p0 = pl.multiple_of(blk * ATTN_BQ, ATTN_BQ)
        k0 = pl.multiple_of(jnp.clip(p0 - ATTN_RADIUS, 0, S - ATTN_KW), ATTN_RADIUS)
        at_start = ((p0 & (L - 1)) == 0).astype(jnp.int32)
        at_end = (((p0 + ATTN_BQ) & (L - 1)) == 0).astype(jnp.int32)
        geom = jnp.where(p0 == 0, 0, jnp.where(p0 == S - ATTN_BQ, 4, 1 + at_start + 2 * at_end))
        s = lax.dot_general(q_ref[pl.ds(p0, ATTN_BQ), :], k_ref[pl.ds(k0, ATTN_KW), :], NT_DIMS,
                            preferred_element_type=F32)
        s = s + bias_ref[geom]
        m = jnp.max(s, axis=-1, keepdims=True)
        p = jnp.exp2(s - m)
        l = jnp.sum(p, axis=-1, keepdims=True)
        acc = jnp.dot(p.astype(BF16), v_ref[pl.ds(k0, ATTN_KW), :], preferred_element_type=F32)
        m = jnp.broadcast_to(m, acc.shape)
        l = jnp.broadcast_to(l, acc.shape)
        nat = natural_rows(p0, ATTN_BQ)
        if not first:
            m_old = m_ref[nat, :]
            m_new = jnp.maximum(m_old, m)
            a_old = jnp.exp2(m_old - m_new)
            a_new = jnp.exp2(m - m_new)
            l = a_old * l_ref[nat, :] + a_new * l
            acc = a_old * acc_ref[nat, :] + a_new * acc
            m = m_new
        if last:
            o_ref[nat, :] = (acc / l).astype(o_ref.dtype)
        else:
            m_ref[nat, :] = m
            l_ref[nat, :] = l
            acc_ref[nat, :] = acc

    unroll = ATTN_UNROLL * 2 if d == 1 else ATTN_UNROLL

    def block_body(i, carry):
        for u in range(unroll):
            one_block(i * unroll + u)
        return carry

    lax.fori_loop(0, S // (ATTN_BQ * unroll), block_body, 0)


def _attn_kernel(*refs):
    n_groups = len(DILATED_GROUPS)
    qkv_refs = refs[:3 * n_groups]
    bias_ref, o_ref, m_ref, l_ref, acc_ref = refs[3 * n_groups:]
    assert DILATED_GROUPS[0][1] == 1
    for step, gi in enumerate(reversed(range(n_groups))):
        window, dilation = DILATED_GROUPS[gi]
        assert window // (2 * dilation) == ATTN_RADIUS
        q_ref, k_ref, v_ref = qkv_refs[3 * gi:3 * gi + 3]
        _attn_group(q_ref, k_ref, v_ref, bias_ref, o_ref, m_ref, l_ref, acc_ref,
                    d=dilation, first=step == 0, last=step == n_groups - 1)


def _attn_call(qkv, B, S, D):
    H = D // HEAD
    head = pl.BlockSpec((S, HEAD), lambda b, h: (b, h))
    return pl.pallas_call(
        _attn_kernel,
        grid=(B, H),
        in_specs=[head] * len(qkv) + [pl.BlockSpec((5, ATTN_BQ, ATTN_KW), lambda b, h: (0, 0, 0))],
        out_specs=head,
        out_shape=jax.ShapeDtypeStruct((B * S, D), BF16),
        scratch_shapes=[pltpu.VMEM((S, HEAD), F32), pltpu.VMEM((S, HEAD), F32), pltpu.VMEM((S, HEAD), F32)],
        compiler_params=_cparams("parallel", "parallel"),
        name="dilated_attention",
    )(*qkv, jnp.asarray(_attn_bias()))


def kernel(x, norm_w, w_in_hgrn, hgrn_lower_bounds, hgrn_gnorm, w_in_attn, w_out, w_ffn_in, w_ffn_out):
    B, S, D = x.shape
    depth = norm_w.shape[0]
    T = B * S
    xf = x.reshape(T, D).astype(F32)
    lb = _lb_call(hgrn_lower_bounds).reshape(2 * depth, 1, D)
    tables = _rope_tables(S)

    hn = _norm_call(xf, w_pre=norm_w[0, 0])
    for layer in range(depth):
        slot = layer // 2
        if layer % 2 == 0:
            proj = _matmul(hn, w_in_hgrn, slot, tm=1024, tn=1024, out_dtype=F32)
            mixed = _hgrn_call(proj, lb, layer, depth, hgrn_gnorm[slot], B, S, D)
        else:
            qkv = []
            for gi in range(len(DILATED_GROUPS)):
                for kind in range(3):
                    tabs = None if kind == 2 else tables[gi, 3 * kind:3 * kind + 3]
                    qkv.append(_qkv_call(hn_by_group[gi], w_in_attn, slot, (gi * 3 + kind) * D, tabs, S,
                                         tm=1024, tn=1024))
            mixed = _attn_call(qkv, B, S, D)
        xf, hn = _outproj_merge(mixed, w_out, layer, xf, norm_w[layer, 1], norm_w[layer, 2], tm=512)
        act = _matmul_swiglu(hn, w_ffn_in, layer, tm=1024, tn=512)
        h = _matmul(act, w_ffn_out, layer, tm=512, tn=512, out_dtype=BF16)
        if layer + 1 < depth:
            dil = tuple(d for _, d in DILATED_GROUPS if d > 1) if (layer + 1) % 2 == 1 else ()
            xf, hn, *hn_perm = _norm_call(xf, h, norm_w[layer, 3], norm_w[layer + 1, 0], dilations=dil, seq=S)
            hn_by_group = [hn] + hn_perm
        else:
            xf = _norm_call(xf, h, norm_w[layer, 3])
    return xf.reshape(B, S, D).astype(x.dtype)
```

```python
import functools
import math

import numpy as np
import jax
import jax.numpy as jnp
from jax import lax
from jax.experimental import pallas as pl
from jax.experimental.pallas import tpu as pltpu

F32 = jnp.float32
BF16 = jnp.bfloat16

EPS = 1e-6
HEAD = 128
DILATED_GROUPS = ((128, 1), (512, 4), (2048, 16))
ROPE_THETA = 500000.0
ROPE_DIM = HEAD // 4
NEG_INF = -1e30
LOG2E = math.log2(math.e)

GLA_CHUNK = 128
ATTN_BQ = 128
ATTN_RADIUS = 64
ATTN_KW = ATTN_BQ + 2 * ATTN_RADIUS
ATTN_UNROLL = 8

VMEM_LIMIT = 56 * 1024 * 1024
MXU_N = 256

IN_PROJ_TILE = (1024, 1024)
SWIGLU_TILE = (2048, 512)
FFN_OUT_TILE = (512, 512)
OUT_PROJ_ROWS = 512

NT_DIMS = (((1,), (1,)), ((), ()))
TN_DIMS = (((0,), (0,)), ((), ()))


def _cparams(*sem):
    return pltpu.CompilerParams(dimension_semantics=sem, vmem_limit_bytes=VMEM_LIMIT)


def _rms(x, w):
    return x * lax.rsqrt(jnp.mean(x * x, axis=-1, keepdims=True) + EPS) * w


def _sigmoid(x):
    return 1.0 / (1.0 + jnp.exp(-x))


def _norm_kernel(*refs, has_h, has_pre, dilations):
    refs = list(refs)
    x_ref = refs.pop(0)
    x = x_ref[...]
    if has_h:
        h_ref = refs.pop(0)
        wpost_ref = refs.pop(0)
    if has_pre:
        wpre_ref = refs.pop(0)
    if has_h:
        xo_ref = refs.pop(0)
        x = x + _rms(h_ref[...].astype(F32), wpost_ref[...])
        xo_ref[...] = x
    if has_pre:
        hn_ref = refs.pop(0)
        hn = _rms(x, wpre_ref[...])
        hn_ref[...] = hn.astype(BF16)
        if dilations:
            slab_ref = refs[-1]
            rows = hn.shape[0]
            n_slab = hn.shape[1] // HEAD
            for c in range(n_slab):
                slab_ref[c] = hn[:, c * HEAD:(c + 1) * HEAD]
            for d, hp_ref in zip(dilations, refs[:len(dilations)]):
                for c in range(n_slab):
                    for r in range(d):
                        hp_ref[r, :, c * HEAD:(c + 1) * HEAD] = (
                            slab_ref[c, pl.ds(r, rows // d, stride=d), :].astype(BF16))


def _norm_call(x, h=None, w_post=None, w_pre=None, rows=256, dilations=(), seq=None):
    T, D = x.shape
    has_h, has_pre = h is not None, w_pre is not None
    row_spec = pl.BlockSpec((rows, D), lambda i: (i, 0))
    w_spec = pl.BlockSpec((1, D), lambda i: (0, 0))
    args, in_specs, out_shape, out_specs, scratch = [x], [row_spec], [], [], []
    if has_h:
        args += [h, w_post.reshape(1, D)]
        in_specs += [row_spec, w_spec]
        out_shape.append(jax.ShapeDtypeStruct((T, D), F32))
        out_specs.append(row_spec)
    if has_pre:
        args.append(w_pre.reshape(1, D))
        in_specs.append(w_spec)
        out_shape.append(jax.ShapeDtypeStruct((T, D), BF16))
        out_specs.append(row_spec)
    if dilations:
        blocks_per_seq = seq // rows
        for d in dilations:
            out_shape.append(jax.ShapeDtypeStruct((T // seq, d, seq // d, D), BF16))
            out_specs.append(pl.BlockSpec((None, d, rows // d, D),
                                          lambda i: (i // blocks_per_seq, 0, i % blocks_per_seq, 0)))
        scratch.append(pltpu.VMEM((D // HEAD, rows, HEAD), F32))
    outs = pl.pallas_call(
        functools.partial(_norm_kernel, has_h=has_h, has_pre=has_pre, dilations=tuple(dilations)),
        grid=(T // rows,),
        in_specs=in_specs,
        out_specs=out_specs,
        out_shape=out_shape,
        scratch_shapes=scratch,
        compiler_params=_cparams("parallel"),
        name="norm_merge",
    )(*args)
    outs = [o.reshape(T, D) for o in outs]
    return outs if len(outs) > 1 else outs[0]


def _mm_kernel(x_ref, w_ref, o_ref, wb_ref):
    @pl.when(pl.program_id(1) == 0)
    def _():
        wb_ref[...] = w_ref[...].astype(BF16)

    o_ref[...] = jnp.dot(x_ref[...], wb_ref[...], preferred_element_type=F32).astype(o_ref.dtype)


def _matmul(x, w, layer, *, tm, tn, out_dtype):
    T, K = x.shape
    N = w.shape[2]
    return pl.pallas_call(
        _mm_kernel,
        grid=(N // tn, T // tm),
        in_specs=[
            pl.BlockSpec((tm, K), lambda j, i: (i, 0)),
            pl.BlockSpec((None, K, tn), lambda j, i: (layer, 0, j)),
        ],
        out_specs=pl.BlockSpec((tm, tn), lambda j, i: (i, j)),
        out_shape=jax.ShapeDtypeStruct((T, N), out_dtype),
        scratch_shapes=[pltpu.VMEM((K, tn), BF16)],
        compiler_params=_cparams("parallel", "arbitrary"),
        name="matmul",
    )(x, w)


def _swiglu_kernel(x_ref, wg_ref, wu_ref, o_ref, wgb_ref, wub_ref):
    @pl.when(pl.program_id(1) == 0)
    def _():
        wgb_ref[...] = wg_ref[...].astype(BF16)
        wub_ref[...] = wu_ref[...].astype(BF16)

    x = x_ref[...]
    gate = jnp.dot(x, wgb_ref[...], preferred_element_type=F32)
    up = jnp.dot(x, wub_ref[...], preferred_element_type=F32)
    o_ref[...] = (gate * _sigmoid(gate) * up).astype(o_ref.dtype)


def _matmul_swiglu(x, w, layer, *, tm, tn):
    T, K = x.shape
    F = w.shape[2] // 2
    nf = F // tn
    return pl.pallas_call(
        _swiglu_kernel,
        grid=(nf, T // tm),
        in_specs=[
            pl.BlockSpec((tm, K), lambda j, i: (i, 0)),
            pl.BlockSpec((None, K, tn), lambda j, i: (layer, 0, j)),
            pl.BlockSpec((None, K, tn), lambda j, i: (layer, 0, j + nf)),
        ],
        out_specs=pl.BlockSpec((tm, tn), lambda j, i: (i, j)),
        out_shape=jax.ShapeDtypeStruct((T, F), BF16),
        scratch_shapes=[pltpu.VMEM((K, tn), BF16), pltpu.VMEM((K, tn), BF16)],
        compiler_params=_cparams("parallel", "arbitrary"),
        name="matmul_swiglu",
    )(x, w, w)


def _outproj_kernel(a_ref, w_ref, x_ref, wpost_ref, wpre_ref, xo_ref, hn_ref, wb_ref):
    @pl.when(pl.program_id(0) == 0)
    def _():
        wb_ref[...] = w_ref[...].astype(BF16)

    h = jnp.dot(a_ref[...], wb_ref[...], preferred_element_type=F32)
    x = x_ref[...] + _rms(h, wpost_ref[...])
    xo_ref[...] = x
    hn_ref[...] = _rms(x, wpre_ref[...]).astype(BF16)


def _outproj_merge(a, w, layer, x, w_post, w_pre, *, tm):
    T, D = x.shape
    row = lambda dt: pl.BlockSpec((tm, D), lambda i: (i, 0))
    vec = pl.BlockSpec((1, D), lambda i: (0, 0))
    return pl.pallas_call(
        _outproj_kernel,
        grid=(T // tm,),
        in_specs=[row(BF16),
                  pl.BlockSpec((None, D, D), lambda i: (layer, 0, 0), pipeline_mode=pl.Buffered(1)),
                  row(F32), vec, vec],
        out_specs=[row(F32), row(BF16)],
        out_shape=[jax.ShapeDtypeStruct((T, D), F32), jax.ShapeDtypeStruct((T, D), BF16)],
        scratch_shapes=[pltpu.VMEM((D, D), BF16)],
        compiler_params=_cparams("arbitrary"),
        name="outproj_merge",
    )(a, w, x, w_post.reshape(1, D), w_pre.reshape(1, D))


def _lb_kernel(x_ref, o_ref, *, depth):
    for d in range(2):
        r = [x_ref[pl.ds(d * depth + i, 1), :] for i in range(depth)]
        m = functools.reduce(jnp.maximum, r)
        e = [jnp.exp(v - m) for v in r]
        tot = functools.reduce(lambda a, b: a + b, e)
        c = None
        first = None
        for i in range(depth):
            p = e[i] / tot
            c = p if c is None else c + p
            if first is None:
                first = c
            o_ref[pl.ds(d * depth + i, 1), :] = c - first


def _lb_call(lower_bounds):
    two, depth, D = lower_bounds.shape
    return pl.pallas_call(
        functools.partial(_lb_kernel, depth=depth),
        out_shape=jax.ShapeDtypeStruct((two * depth, D), F32),
        name="hgrn_lower_bounds",
    )(lower_bounds.reshape(two * depth, D).astype(F32))


def _gla_level_ids(C, rev):
    t = np.arange(C)[:, None]
    s = np.arange(C)[None, :]
    x = t ^ s
    lv = np.where(x > 0, np.floor(np.log2(np.maximum(x, 1))).astype(np.int64), int(math.log2(C)))
    allowed = (t <= s) if rev else (t >= s)
    return np.where(allowed, lv, -1).astype(np.int32)


def _gla_tri(C, rev):
    t = np.arange(C)[:, None]
    j = np.arange(C)[None, :]
    return ((j >= t) if rev else (j <= t)).astype(np.float32)


def _split_row_exponent(b, w, rev):
    C = b.shape[0]
    span = 2 * w
    mid = w if rev else w - 1
    pieces = []
    for a in range(C // span):
        lo, split, hi = a * span, a * span + w, (a + 1) * span
        row = b[lo + mid:lo + mid + 1, :]
        if w % 8 == 0:
            first, second = b[lo:split, :], b[split:hi, :]
            pieces += [first - row, row - second] if rev else [row - first, second - row]
        else:
            pieces.append(-jnp.abs(b[lo:hi, :] - row))
    return pieces[0] if len(pieces) == 1 else jnp.concatenate(pieces, axis=0)


def _fine_decay(fg, w, rev):
    C = fg.shape[0]
    t4 = lax.broadcasted_iota(jnp.int32, fg.shape, 0) & 3
    if w == 1:
        keep = (t4 & 1) == (0 if rev else 1)
        return jnp.where(keep, fg, 1.0)
    nxt = pltpu.roll(fg, C - 1, 0)
    prv = pltpu.roll(fg, 1, 0)
    by_row = [fg * nxt, fg, 1.0, prv] if rev else [nxt, 1.0, fg, fg * prv]
    return jnp.where(t4 == 0, by_row[0], jnp.where(t4 == 1, by_row[1], jnp.where(t4 == 2, by_row[2], by_row[3])))


def _gla_chunk(qb, f_raw, vb, lb, st_ref, lv_ref, tri_ref, rev):
    C = qb.shape[0]
    nlev = int(math.log2(C))
    fg = lb + (1.0 - lb) * _sigmoid(f_raw)
    kb = (1.0 - fg).astype(BF16)
    lg = jnp.log(fg) * LOG2E
    hi = lg.astype(BF16)
    lo = (lg - hi.astype(F32)).astype(BF16)
    tri = tri_ref[...]
    b = jnp.dot(tri, hi, preferred_element_type=F32) + jnp.dot(tri, lo, preferred_element_type=F32)

    lv = lv_ref[...]
    scores = jnp.where(lv == nlev, lax.dot_general(qb, kb, NT_DIMS, preferred_element_type=F32), 0.0)
    for p in range(nlev):
        w = 1 << p
        if w < 4:
            z = _fine_decay(fg, w, rev).astype(BF16)
        else:
            z = jnp.exp2(_split_row_exponent(b, w, rev)).astype(BF16)
        s = lax.dot_general(qb * z, kb * z, NT_DIMS, preferred_element_type=F32)
        scores = jnp.where(lv == p, s, scores)

    b_edge = b[0:1, :] if rev else b[C - 1:C, :]
    q_st = qb * jnp.exp2(b).astype(BF16)
    k_st = kb * jnp.exp2(b_edge - b).astype(BF16)
    st = st_ref[...]
    o = (jnp.dot(scores.astype(BF16), vb, preferred_element_type=F32)
         + lax.dot_general(q_st, st.astype(BF16), NT_DIMS, preferred_element_type=F32))
    st_ref[...] = st * jnp.exp2(b_edge) + lax.dot_general(vb, k_st, TN_DIMS, preferred_element_type=F32)
    return o


def _hgrn_kernel(q_ref, ff_ref, fb_ref, i_ref, g_ref, lbf_ref, lbb_ref, gn_ref,
                 lvf_ref, lvb_ref, trif_ref, trib_ref, o_ref,
                 qs_ref, vs_ref, of_ref, ob_ref, stf_ref, stb_ref, *, chunk):
    S = q_ref.shape[0]
    n_chunks = S // chunk

    def chunk_rows(c):
        return pl.ds(pl.multiple_of(c * chunk, chunk), chunk)

    def stage_body(c, carry):
        rows = chunk_rows(c)
        qr = q_ref[rows, :]
        qs_ref[rows, :] = (qr * _sigmoid(qr)).astype(BF16)
        vs_ref[rows, :] = i_ref[rows, :].astype(BF16)
        return carry

    lax.fori_loop(0, n_chunks, stage_body, 0, unroll=4)
    stf_ref[...] = jnp.zeros_like(stf_ref)
    stb_ref[...] = jnp.zeros_like(stb_ref)

    def scan_body(j, carry):
        rf = chunk_rows(j)
        rb = chunk_rows(n_chunks - 1 - j)
        of_ref[rf, :] = _gla_chunk(qs_ref[rf, :], ff_ref[rf, :], vs_ref[rf, :], lbf_ref[...],
                                   stf_ref, lvf_ref, trif_ref, False)
        ob_ref[rb, :] = _gla_chunk(qs_ref[rb, :], fb_ref[rb, :], vs_ref[rb, :], lbb_ref[...],
                                   stb_ref, lvb_ref, trib_ref, True)
        return carry

    lax.fori_loop(0, n_chunks, scan_body, 0, unroll=8)

    def out_body(c, carry):
        rows = chunk_rows(c)
        o = of_ref[rows, :] + ob_ref[rows, :]
        gate = g_ref[rows, :]
        o_ref[rows, :] = (_rms(o, gn_ref[...]) * (gate * _sigmoid(gate))).astype(o_ref.dtype)
        return carry

    lax.fori_loop(0, n_chunks, out_body, 0, unroll=4)


def _hgrn_call(proj, lb, layer, depth, gnorm, B, S, D):
    H = D // HEAD
    C = GLA_CHUNK
    col = lambda c: pl.BlockSpec((S, HEAD), lambda b, h, c=c: (b, c * H + h))
    lb_spec = lambda r: pl.BlockSpec((None, 1, HEAD), lambda b, h, r=r: (r, 0, h))
    const = lambda shape: pl.BlockSpec(shape, lambda b, h: (0, 0))
    return pl.pallas_call(
        functools.partial(_hgrn_kernel, chunk=C),
        grid=(B, H),
        in_specs=[col(0), col(1), col(2), col(3), col(4),
                  lb_spec(layer), lb_spec(depth + layer), const((1, HEAD)),
                  const((C, C)), const((C, C)), const((C, C)), const((C, C))],
        out_specs=pl.BlockSpec((S, HEAD), lambda b, h: (b, h)),
        out_shape=jax.ShapeDtypeStruct((B * S, D), BF16),
        scratch_shapes=[pltpu.VMEM((S, HEAD), BF16), pltpu.VMEM((S, HEAD), BF16),
                        pltpu.VMEM((S, HEAD), F32), pltpu.VMEM((S, HEAD), F32),
                        pltpu.VMEM((HEAD, HEAD), F32), pltpu.VMEM((HEAD, HEAD), F32)],
        compiler_params=_cparams("parallel", "parallel"),
        name="hgrn2_mixer",
    )(proj, proj, proj, proj, proj, lb, lb, gnorm.reshape(1, HEAD),
      jnp.asarray(_gla_level_ids(C, False)), jnp.asarray(_gla_level_ids(C, True)),
      jnp.asarray(_gla_tri(C, False), BF16), jnp.asarray(_gla_tri(C, True), BF16))


def _rope_tables(S):
    half = ROPE_DIM // 2
    pos = jnp.arange(S, dtype=F32)
    inv_freq = ROPE_THETA ** (-(jnp.arange(0, ROPE_DIM, 2, dtype=F32) / ROPE_DIM))
    ang = pos[:, None] * inv_freq[None, :]
    ang = jnp.concatenate([ang, ang], axis=-1)
    cos, sin = jnp.cos(ang), jnp.sin(ang)
    pad = lambda t, val: jnp.concatenate([t, jnp.full((S, HEAD - t.shape[1]), val, F32)], axis=-1)
    zeros = jnp.zeros((S, half), F32)
    k_tabs = [pad(cos, 1.0),
              pad(jnp.concatenate([-sin[:, :half], zeros], axis=-1), 0.0),
              pad(jnp.concatenate([zeros, sin[:, half:]], axis=-1), 0.0)]
    qs = HEAD ** -0.5 * LOG2E
    tabs = jnp.stack([t * qs for t in k_tabs] + k_tabs)
    return jnp.stack([tabs.reshape(6, S // d, d, HEAD).transpose(0, 2, 1, 3).reshape(6, S, HEAD)
                      for _, d in DILATED_GROUPS])


def _qkv_kernel(x_ref, w_ref, tab_ref, o_ref, wb_ref, *, rope):
    @pl.when(pl.program_id(1) == 0)
    def _():
        wb_ref[...] = w_ref[...].astype(BF16)

    tn = wb_ref.shape[1]
    half = ROPE_DIM // 2
    x = x_ref[...]
    for nt in range(tn // MXU_N):
        res = jnp.dot(x, wb_ref[:, nt * MXU_N:(nt + 1) * MXU_N], preferred_element_type=F32)
        for s in range(MXU_N // HEAD):
            y = res[:, s * HEAD:(s + 1) * HEAD]
            if rope:
                y = (y * tab_ref[0] + pltpu.roll(y, HEAD - half, 1) * tab_ref[1]
                     + pltpu.roll(y, half, 1) * tab_ref[2])
            lane0 = nt * MXU_N + s * HEAD
            o_ref[:, lane0:lane0 + HEAD] = y.astype(BF16)


def _qkv_call(x, w, layer, col0, tables, S, *, tm, tn):
    T, K = x.shape
    rope = tables is not None
    if not rope:
        tables = jnp.zeros((3, tm, HEAD), F32)
    blocks_per_seq = S // tm
    return pl.pallas_call(
        functools.partial(_qkv_kernel, rope=rope),
        grid=(K // tn, T // tm),
        in_specs=[
            pl.BlockSpec((tm, K), lambda j, i: (i, 0)),
            pl.BlockSpec((None, K, tn), lambda j, i: (layer, 0, col0 // tn + j)),
            pl.BlockSpec((3, tm, HEAD), (lambda j, i: (0, i % blocks_per_seq, 0)) if rope else (lambda j, i: (0, 0, 0))),
        ],
        out_specs=pl.BlockSpec((tm, tn), lambda j, i: (i, j)),
        out_shape=jax.ShapeDtypeStruct((T, K), BF16),
        scratch_shapes=[pltpu.VMEM((K, tn), BF16)],
        compiler_params=_cparams("parallel", "arbitrary"),
        name="qkv_proj",
    )(x, w, tables)


def _attn_bias():
    i = np.arange(ATTN_BQ)[:, None]
    j = np.arange(ATTN_KW)[None, :]
    band = lambda off: np.abs(j - i - off) <= ATTN_RADIUS
    r = ATTN_RADIUS
    masks = [band(0), band(r), band(r) & (j >= r), band(r) & (j < r + ATTN_BQ), band(2 * r)]
    return np.where(np.stack(masks), 0.0, NEG_INF).astype(np.float32)


def _attn_group(q_ref, k_ref, v_ref, bias_ref, o_ref, m_ref, l_ref, acc_ref, *, d, first, last):
    S = q_ref.shape[0]
    L = S // d
    assert L >= ATTN_KW and L % ATTN_BQ == 0
    log_l = int(math.log2(L))

    def natural_rows(p0, n):
        if d == 1:
            return pl.ds(p0, n)
        return pl.ds((p0 >> log_l) + d * (p0 & (L - 1)), n, stride=d)

    def one_block(blk):
        p0 = pl.multiple_of(blk * ATTN_BQ, ATTN_BQ)
        k0 = pl.multiple_of(jnp.clip(p0 - ATTN_RADIUS, 0, S - ATTN_KW), ATTN_RADIUS)
        at_start = ((p0 & (L - 1)) == 0).astype(jnp.int32)
        at_end = (((p0 + ATTN_BQ) & (L - 1)) == 0).astype(jnp.int32)
        geom = jnp.where(p0 == 0, 0, jnp.where(p0 == S - ATTN_BQ, 4, 1 + at_start + 2 * at_end))
        s = lax.dot_general(q_ref[pl.ds(p0, ATTN_BQ), :], k_ref[pl.ds(k0, ATTN_KW), :], NT_DIMS,
                            preferred_element_type=F32)
        s = s + bias_ref[geom]
        m = jnp.max(s, axis=-1, keepdims=True)
        p = jnp.exp2(s - m)
        l = jnp.sum(p, axis=-1, keepdims=True)
        acc = jnp.dot(p.astype(BF16), v_ref[pl.ds(k0, ATTN_KW), :], preferred_element_type=F32)
        m = jnp.broadcast_to(m, acc.shape)
        l = jnp.broadcast_to(l, acc.shape)
        nat = natural_rows(p0, ATTN_BQ)
        if not first:
            m_old = m_ref[nat, :]
            m_new = jnp.maximum(m_old, m)
            a_old = jnp.exp2(m_old - m_new)
            a_new = jnp.exp2(m - m_new)
            l = a_old * l_ref[nat, :] + a_new * l
            acc = a_old * acc_ref[nat, :] + a_new * acc
            m = m_new
        if last:
            o_ref[nat, :] = (acc / l).astype(o_ref.dtype)
        else:
            m_ref[nat, :] = m
            l_ref[nat, :] = l
            acc_ref[nat, :] = acc

    unroll = ATTN_UNROLL * 2 if d == 1 else ATTN_UNROLL

    def block_body(i, carry):
        for u in range(unroll):
            one_block(i * unroll + u)
        return carry

    lax.fori_loop(0, S // (ATTN_BQ * unroll), block_body, 0)


def _attn_kernel(*refs):
    n_groups = len(DILATED_GROUPS)
    qkv_refs = refs[:3 * n_groups]
    bias_ref, o_ref, m_ref, l_ref, acc_ref = refs[3 * n_groups:]
    assert DILATED_GROUPS[0][1] == 1
    for step, gi in enumerate(reversed(range(n_groups))):
        window, dilation = DILATED_GROUPS[gi]
        assert window // (2 * dilation) == ATTN_RADIUS
        q_ref, k_ref, v_ref = qkv_refs[3 * gi:3 * gi + 3]
        _attn_group(q_ref, k_ref, v_ref, bias_ref, o_ref, m_ref, l_ref, acc_ref,
                    d=dilation, first=step == 0, last=step == n_groups - 1)


def _attn_call(qkv, B, S, D):
    H = D // HEAD
    head = pl.BlockSpec((S, HEAD), lambda b, h: (b, h))
    return pl.pallas_call(
        _attn_kernel,
        grid=(B, H),
        in_specs=[head] * len(qkv) + [pl.BlockSpec((5, ATTN_BQ, ATTN_KW), lambda b, h: (0, 0, 0))],
        out_specs=head,
        out_shape=jax.ShapeDtypeStruct((B * S, D), BF16),
        scratch_shapes=[pltpu.VMEM((S, HEAD), F32), pltpu.VMEM((S, HEAD), F32), pltpu.VMEM((S, HEAD), F32)],
        compiler_params=_cparams("parallel", "parallel"),
        name="dilated_attention",
    )(*qkv, jnp.asarray(_attn_bias()))


def kernel(x, norm_w, w_in_hgrn, hgrn_lower_bounds, hgrn_gnorm, w_in_attn, w_out, w_ffn_in, w_ffn_out):
    B, S, D = x.shape
    depth = norm_w.shape[0]
    T = B * S
    xf = x.reshape(T, D).astype(F32)
    lb = _lb_call(hgrn_lower_bounds).reshape(2 * depth, 1, D)
    tables = _rope_tables(S)

    hn = _norm_call(xf, w_pre=norm_w[0, 0])
    for layer in range(depth):
        slot = layer // 2
        if layer % 2 == 0:
            proj = _matmul(hn, w_in_hgrn, slot, tm=IN_PROJ_TILE[0], tn=IN_PROJ_TILE[1], out_dtype=F32)
            mixed = _hgrn_call(proj, lb, layer, depth, hgrn_gnorm[slot], B, S, D)
        else:
            qkv = []
            for gi in range(len(DILATED_GROUPS)):
                for kind in range(3):
                    tabs = None if kind == 2 else tables[gi, 3 * kind:3 * kind + 3]
                    qkv.append(_qkv_call(hn_by_group[gi], w_in_attn, slot, (gi * 3 + kind) * D, tabs, S,
                                         tm=IN_PROJ_TILE[0], tn=IN_PROJ_TILE[1]))
            mixed = _attn_call(qkv, B, S, D)
        xf, hn = _outproj_merge(mixed, w_out, layer, xf, norm_w[layer, 1], norm_w[layer, 2], tm=OUT_PROJ_ROWS)
        act = _matmul_swiglu(hn, w_ffn_in, layer, tm=SWIGLU_TILE[0], tn=SWIGLU_TILE[1])
        h = _matmul(act, w_ffn_out, layer, tm=FFN_OUT_TILE[0], tn=FFN_OUT_TILE[1], out_dtype=BF16)
        if layer + 1 < depth:
            dil = tuple(d for _, d in DILATED_GROUPS if d > 1) if (layer + 1) % 2 == 1 else ()
            xf, hn, *hn_perm = _norm_call(xf, h, norm_w[layer, 3], norm_w[layer + 1, 0], dilations=dil, seq=S)
            hn_by_group = [hn] + hn_perm
        else:
            xf = _norm_call(xf, h, norm_w[layer, 3])
    return xf.reshape(B, S, D).astype(x.dtype)
```

```python
import functools
import math

import numpy as np
import jax
import jax.numpy as jnp
from jax import lax
from jax.experimental import pallas as pl
from jax.experimental.pallas import tpu as pltpu

F32 = jnp.float32
BF16 = jnp.bfloat16

EPS = 1e-6
HEAD = 128
DILATED_GROUPS = ((128, 1), (512, 4), (2048, 16))
ROPE_THETA = 500000.0
ROPE_DIM = HEAD // 4
NEG_INF = -1e30
LOG2E = math.log2(math.e)

GLA_CHUNK = 128
ATTN_BQ = 128
ATTN_RADIUS = 64
ATTN_KW = ATTN_BQ + 2 * ATTN_RADIUS
ATTN_UNROLL = 16

VMEM_LIMIT = 56 * 1024 * 1024
MXU_N = 256

IN_PROJ_TILE = (1024, 1024)
SWIGLU_TILE = (1024, 512)
FFN_OUT_TILE = (512, 512)
OUT_PROJ_ROWS = 512

NT_DIMS = (((1,), (1,)), ((), ()))
TN_DIMS = (((0,), (0,)), ((), ()))


def _cparams(*sem):
    return pltpu.CompilerParams(dimension_semantics=sem, vmem_limit_bytes=VMEM_LIMIT)


def _rms(x, w):
    return x * lax.rsqrt(jnp.mean(x * x, axis=-1, keepdims=True) + EPS) * w


def _sigmoid(x):
    return 1.0 / (1.0 + jnp.exp(-x))


def _norm_kernel(*refs, has_h, has_pre, dilations):
    refs = list(refs)
    x_ref = refs.pop(0)
    x = x_ref[...]
    if has_h:
        h_ref = refs.pop(0)
        wpost_ref = refs.pop(0)
    if has_pre:
        wpre_ref = refs.pop(0)
    if has_h:
        xo_ref = refs.pop(0)
        x = x + _rms(h_ref[...].astype(F32), wpost_ref[...])
        xo_ref[...] = x
    if has_pre:
        hn_ref = refs.pop(0)
        hn = _rms(x, wpre_ref[...])
        hn_ref[...] = hn.astype(BF16)
        if dilations:
            slab_ref = refs[-1]
            rows = hn.shape[0]
            n_slab = hn.shape[1] // HEAD
            for c in range(n_slab):
                slab_ref[c] = hn[:, c * HEAD:(c + 1) * HEAD]
            for d, hp_ref in zip(dilations, refs[:len(dilations)]):
                for c in range(n_slab):
                    for r in range(d):
                        hp_ref[r, :, c * HEAD:(c + 1) * HEAD] = (
                            slab_ref[c, pl.ds(r, rows // d, stride=d), :].astype(BF16))


def _norm_call(x, h=None, w_post=None, w_pre=None, rows=256, dilations=(), seq=None):
    T, D = x.shape
    has_h, has_pre = h is not None, w_pre is not None
    row_spec = pl.BlockSpec((rows, D), lambda i: (i, 0))
    w_spec = pl.BlockSpec((1, D), lambda i: (0, 0))
    args, in_specs, out_shape, out_specs, scratch = [x], [row_spec], [], [], []
    if has_h:
        args += [h, w_post.reshape(1, D)]
        in_specs += [row_spec, w_spec]
        out_shape.append(jax.ShapeDtypeStruct((T, D), F32))
        out_specs.append(row_spec)
    if has_pre:
        args.append(w_pre.reshape(1, D))
        in_specs.append(w_spec)
        out_shape.append(jax.ShapeDtypeStruct((T, D), BF16))
        out_specs.append(row_spec)
    if dilations:
        blocks_per_seq = seq // rows
        for d in dilations:
            out_shape.append(jax.ShapeDtypeStruct((T // seq, d, seq // d, D), BF16))
            out_specs.append(pl.BlockSpec((None, d, rows // d, D),
                                          lambda i: (i // blocks_per_seq, 0, i % blocks_per_seq, 0)))
        scratch.append(pltpu.VMEM((D // HEAD, rows, HEAD), F32))
    outs = pl.pallas_call(
        functools.partial(_norm_kernel, has_h=has_h, has_pre=has_pre, dilations=tuple(dilations)),
        grid=(T // rows,),
        in_specs=in_specs,
        out_specs=out_specs,
        out_shape=out_shape,
        scratch_shapes=scratch,
        compiler_params=_cparams("parallel"),
        name="norm_merge",
    )(*args)
    outs = [o.reshape(T, D) for o in outs]
    return outs if len(outs) > 1 else outs[0]


def _mm_kernel(x_ref, w_ref, o_ref, wb_ref):
    @pl.when(pl.program_id(1) == 0)
    def _():
        wb_ref[...] = w_ref[...].astype(BF16)

    o_ref[...] = jnp.dot(x_ref[...], wb_ref[...], preferred_element_type=F32).astype(o_ref.dtype)


def _matmul(x, w, layer, *, tm, tn, out_dtype):
    T, K = x.shape
    N = w.shape[2]
    return pl.pallas_call(
        _mm_kernel,
        grid=(N // tn, T // tm),
        in_specs=[
            pl.BlockSpec((tm, K), lambda j, i: (i, 0)),
            pl.BlockSpec((None, K, tn), lambda j, i: (layer, 0, j)),
        ],
        out_specs=pl.BlockSpec((tm, tn), lambda j, i: (i, j)),
        out_shape=jax.ShapeDtypeStruct((T, N), out_dtype),
        scratch_shapes=[pltpu.VMEM((K, tn), BF16)],
        compiler_params=_cparams("parallel", "arbitrary"),
        name="matmul",
    )(x, w)


def _swiglu_kernel(x_ref, wg_ref, wu_ref, o_ref, wgb_ref, wub_ref):
    @pl.when(pl.program_id(1) == 0)
    def _():
        wgb_ref[...] = wg_ref[...].astype(BF16)
        wub_ref[...] = wu_ref[...].astype(BF16)

    x = x_ref[...]
    gate = jnp.dot(x, wgb_ref[...], preferred_element_type=F32)
    up = jnp.dot(x, wub_ref[...], preferred_element_type=F32)
    o_ref[...] = (gate * _sigmoid(gate) * up).astype(o_ref.dtype)


def _matmul_swiglu(x, w, layer, *, tm, tn):
    T, K = x.shape
    F = w.shape[2] // 2
    nf = F // tn
    return pl.pallas_call(
        _swiglu_kernel,
        grid=(nf, T // tm),
        in_specs=[
            pl.BlockSpec((tm, K), lambda j, i: (i, 0)),
            pl.BlockSpec((None, K, tn), lambda j, i: (layer, 0, j)),
            pl.BlockSpec((None, K, tn), lambda j, i: (layer, 0, j + nf)),
        ],
        out_specs=pl.BlockSpec((tm, tn), lambda j, i: (i, j)),
        out_shape=jax.ShapeDtypeStruct((T, F), BF16),
        scratch_shapes=[pltpu.VMEM((K, tn), BF16), pltpu.VMEM((K, tn), BF16)],
        compiler_params=_cparams("parallel", "arbitrary"),
        name="matmul_swiglu",
    )(x, w, w)


def _outproj_kernel(a_ref, w_ref, x_ref, wpost_ref, wpre_ref, xo_ref, hn_ref, wb_ref):
    @pl.when(pl.program_id(0) == 0)
    def _():
        wb_ref[...] = w_ref[...].astype(BF16)

    h = jnp.dot(a_ref[...], wb_ref[...], preferred_element_type=F32)
    x = x_ref[...] + _rms(h, wpost_ref[...])
    xo_ref[...] = x
    hn_ref[...] = _rms(x, wpre_ref[...]).astype(BF16)


def _outproj_merge(a, w, layer, x, w_post, w_pre, *, tm):
    T, D = x.shape
    row = lambda dt: pl.BlockSpec((tm, D), lambda i: (i, 0))
    vec = pl.BlockSpec((1, D), lambda i: (0, 0))
    return pl.pallas_call(
        _outproj_kernel,
        grid=(T // tm,),
        in_specs=[row(BF16),
                  pl.BlockSpec((None, D, D), lambda i: (layer, 0, 0), pipeline_mode=pl.Buffered(1)),
                  row(F32), vec, vec],
        out_specs=[row(F32), row(BF16)],
        out_shape=[jax.ShapeDtypeStruct((T, D), F32), jax.ShapeDtypeStruct((T, D), BF16)],
        scratch_shapes=[pltpu.VMEM((D, D), BF16)],
        compiler_params=_cparams("arbitrary"),
        name="outproj_merge",
    )(a, w, x, w_post.reshape(1, D), w_pre.reshape(1, D))


def _lb_kernel(x_ref, o_ref, *, depth):
    for d in range(2):
        r = [x_ref[pl.ds(d * depth + i, 1), :] for i in range(depth)]
        m = functools.reduce(jnp.maximum, r)
        e = [jnp.exp(v - m) for v in r]
        tot = functools.reduce(lambda a, b: a + b, e)
        c = None
        first = None
        for i in range(depth):
            p = e[i] / tot
            c = p if c is None else c + p
            if first is None:
                first = c
            o_ref[pl.ds(d * depth + i, 1), :] = c - first


def _lb_call(lower_bounds):
    two, depth, D = lower_bounds.shape
    return pl.pallas_call(
        functools.partial(_lb_kernel, depth=depth),
        out_shape=jax.ShapeDtypeStruct((two * depth, D), F32),
        name="hgrn_lower_bounds",
    )(lower_bounds.reshape(two * depth, D).astype(F32))


def _gla_level_ids(C, rev):
    t = np.arange(C)[:, None]
    s = np.arange(C)[None, :]
    x = t ^ s
    lv = np.where(x > 0, np.floor(np.log2(np.maximum(x, 1))).astype(np.int64), int(math.log2(C)))
    allowed = (t <= s) if rev else (t >= s)
    return np.where(allowed, lv, -1).astype(np.int32)


def _gla_tri(C, rev):
    t = np.arange(C)[:, None]
    j = np.arange(C)[None, :]
    return ((j >= t) if rev else (j <= t)).astype(np.float32)


def _split_row_exponent(b, w, rev):
    C = b.shape[0]
    span = 2 * w
    mid = w if rev else w - 1
    pieces = []
    for a in range(C // span):
        lo, split, hi = a * span, a * span + w, (a + 1) * span
        row = b[lo + mid:lo + mid + 1, :]
        if w % 8 == 0:
            first, second = b[lo:split, :], b[split:hi, :]
            pieces += [first - row, row - second] if rev else [row - first, second - row]
        else:
            pieces.append(-jnp.abs(b[lo:hi, :] - row))
    return pieces[0] if len(pieces) == 1 else jnp.concatenate(pieces, axis=0)


def _fine_decay(fg, w, rev):
    C = fg.shape[0]
    t4 = lax.broadcasted_iota(jnp.int32, fg.shape, 0) & 3
    if w == 1:
        keep = (t4 & 1) == (0 if rev else 1)
        return jnp.where(keep, fg, 1.0)
    nxt = pltpu.roll(fg, C - 1, 0)
    prv = pltpu.roll(fg, 1, 0)
    by_row = [fg * nxt, fg, 1.0, prv] if rev else [nxt, 1.0, fg, fg * prv]
    return jnp.where(t4 == 0, by_row[0], jnp.where(t4 == 1, by_row[1], jnp.where(t4 == 2, by_row[2], by_row[3])))


def _gla_chunk(qb, f_raw, vb, lb, st_ref, lv_ref, tri_ref, rev):
    C = qb.shape[0]
    nlev = int(math.log2(C))
    fg = lb + (1.0 - lb) * _sigmoid(f_raw)
    kb = (1.0 - fg).astype(BF16)
    lg = jnp.log(fg) * LOG2E
    hi = lg.astype(BF16)
    lo = (lg - hi.astype(F32)).astype(BF16)
    tri = tri_ref[...]
    b = jnp.dot(tri, hi, preferred_element_type=F32) + jnp.dot(tri, lo, preferred_element_type=F32)

    lv = lv_ref[...]
    scores = jnp.where(lv == nlev, lax.dot_general(qb, kb, NT_DIMS, preferred_element_type=F32), 0.0)
    for p in range(nlev):
        w = 1 << p
        if w < 4:
            z = _fine_decay(fg, w, rev).astype(BF16)
        else:
            z = jnp.exp2(_split_row_exponent(b, w, rev)).astype(BF16)
        s = lax.dot_general(qb * z, kb * z, NT_DIMS, preferred_element_type=F32)
        scores = jnp.where(lv == p, s, scores)

    b_edge = b[0:1, :] if rev else b[C - 1:C, :]
    q_st = qb * jnp.exp2(b).astype(BF16)
    k_st = kb * jnp.exp2(b_edge - b).astype(BF16)
    st = st_ref[...]
    o = (jnp.dot(scores.astype(BF16), vb, preferred_element_type=F32)
         + lax.dot_general(q_st, st.astype(BF16), NT_DIMS, preferred_element_type=F32))
    st_ref[...] = st * jnp.exp2(b_edge) + lax.dot_general(vb, k_st, TN_DIMS, preferred_element_type=F32)
    return o


def _hgrn_kernel(q_ref, ff_ref, fb_ref, i_ref, g_ref, lbf_ref, lbb_ref, gn_ref,
                 lvf_ref, lvb_ref, trif_ref, trib_ref, o_ref,
                 qs_ref, vs_ref, of_ref, ob_ref, stf_ref, stb_ref, *, chunk):
    S = q_ref.shape[0]
    n_chunks = S // chunk

    def chunk_rows(c):
        return pl.ds(pl.multiple_of(c * chunk, chunk), chunk)

    def stage_body(c, carry):
        rows = chunk_rows(c)
        qr = q_ref[rows, :]
        qs_ref[rows, :] = (qr * _sigmoid(qr)).astype(BF16)
        vs_ref[rows, :] = i_ref[rows, :].astype(BF16)
        return carry

    lax.fori_loop(0, n_chunks, stage_body, 0, unroll=4)
    stf_ref[...] = jnp.zeros_like(stf_ref)
    stb_ref[...] = jnp.zeros_like(stb_ref)

    def scan_body(j, carry):
        rf = chunk_rows(j)
        rb = chunk_rows(n_chunks - 1 - j)
        of_ref[rf, :] = _gla_chunk(qs_ref[rf, :], ff_ref[rf, :], vs_ref[rf, :], lbf_ref[...],
                                   stf_ref, lvf_ref, trif_ref, False)
        ob_ref[rb, :] = _gla_chunk(qs_ref[rb, :], fb_ref[rb, :], vs_ref[rb, :], lbb_ref[...],
                                   stb_ref, lvb_ref, trib_ref, True)
        return carry

    lax.fori_loop(0, n_chunks, scan_body, 0, unroll=8)

    def out_body(c, carry):
        rows = chunk_rows(c)
        o = of_ref[rows, :] + ob_ref[rows, :]
        gate = g_ref[rows, :]
        o_ref[rows, :] = (_rms(o, gn_ref[...]) * (gate * _sigmoid(gate))).astype(o_ref.dtype)
        return carry

    lax.fori_loop(0, n_chunks, out_body, 0, unroll=4)


def _hgrn_call(proj, lb, layer, depth, gnorm, B, S, D):
    H = D // HEAD
    C = GLA_CHUNK
    col = lambda c: pl.BlockSpec((S, HEAD), lambda b, h, c=c: (b, c * H + h))
    lb_spec = lambda r: pl.BlockSpec((None, 1, HEAD), lambda b, h, r=r: (r, 0, h))
    const = lambda shape: pl.BlockSpec(shape, lambda b, h: (0, 0))
    return pl.pallas_call(
        functools.partial(_hgrn_kernel, chunk=C),
        grid=(B, H),
        in_specs=[col(0), col(1), col(2), col(3), col(4),
                  lb_spec(layer), lb_spec(depth + layer), const((1, HEAD)),
                  const((C, C)), const((C, C)), const((C, C)), const((C, C))],
        out_specs=pl.BlockSpec((S, HEAD), lambda b, h: (b, h)),
        out_shape=jax.ShapeDtypeStruct((B * S, D), BF16),
        scratch_shapes=[pltpu.VMEM((S, HEAD), BF16), pltpu.VMEM((S, HEAD), BF16),
                        pltpu.VMEM((S, HEAD), F32), pltpu.VMEM((S, HEAD), F32),
                        pltpu.VMEM((HEAD, HEAD), F32), pltpu.VMEM((HEAD, HEAD), F32)],
        compiler_params=_cparams("parallel", "parallel"),
        name="hgrn2_mixer",
    )(proj, proj, proj, proj, proj, lb, lb, gnorm.reshape(1, HEAD),
      jnp.asarray(_gla_level_ids(C, False)), jnp.asarray(_gla_level_ids(C, True)),
      jnp.asarray(_gla_tri(C, False), BF16), jnp.asarray(_gla_tri(C, True), BF16))


def _rope_tables(S):
    half = ROPE_DIM // 2
    pos = jnp.arange(S, dtype=F32)
    inv_freq = ROPE_THETA ** (-(jnp.arange(0, ROPE_DIM, 2, dtype=F32) / ROPE_DIM))
    ang = pos[:, None] * inv_freq[None, :]
    ang = jnp.concatenate([ang, ang], axis=-1)
    cos, sin = jnp.cos(ang), jnp.sin(ang)
    pad = lambda t, val: jnp.concatenate([t, jnp.full((S, HEAD - t.shape[1]), val, F32)], axis=-1)
    zeros = jnp.zeros((S, half), F32)
    k_tabs = [pad(cos, 1.0),
              pad(jnp.concatenate([-sin[:, :half], zeros], axis=-1), 0.0),
              pad(jnp.concatenate([zeros, sin[:, half:]], axis=-1), 0.0)]
    qs = HEAD ** -0.5 * LOG2E
    tabs = jnp.stack([t * qs for t in k_tabs] + k_tabs)
    return jnp.stack([tabs.reshape(6, S // d, d, HEAD).transpose(0, 2, 1, 3).reshape(6, S, HEAD)
                      for _, d in DILATED_GROUPS])


def _qkv_kernel(x_ref, w_ref, tab_ref, o_ref, wb_ref, *, rope):
    @pl.when(pl.program_id(1) == 0)
    def _():
        wb_ref[...] = w_ref[...].astype(BF16)

    tn = wb_ref.shape[1]
    half = ROPE_DIM // 2
    x = x_ref[...]
    for nt in range(tn // MXU_N):
        res = jnp.dot(x, wb_ref[:, nt * MXU_N:(nt + 1) * MXU_N], preferred_element_type=F32)
        for s in range(MXU_N // HEAD):
            y = res[:, s * HEAD:(s + 1) * HEAD]
            if rope:
                y = (y * tab_ref[0] + pltpu.roll(y, HEAD - half, 1) * tab_ref[1]
                     + pltpu.roll(y, half, 1) * tab_ref[2])
            lane0 = nt * MXU_N + s * HEAD
            o_ref[:, lane0:lane0 + HEAD] = y.astype(BF16)


def _qkv_call(x, w, layer, col0, tables, S, *, tm, tn):
    T, K = x.shape
    rope = tables is not None
    if not rope:
        tables = jnp.zeros((3, tm, HEAD), F32)
    blocks_per_seq = S // tm
    return pl.pallas_call(
        functools.partial(_qkv_kernel, rope=rope),
        grid=(K // tn, T // tm),
        in_specs=[
            pl.BlockSpec((tm, K), lambda j, i: (i, 0)),
            pl.BlockSpec((None, K, tn), lambda j, i: (layer, 0, col0 // tn + j)),
            pl.BlockSpec((3, tm, HEAD), (lambda j, i: (0, i % blocks_per_seq, 0)) if rope else (lambda j, i: (0, 0, 0))),
        ],
        out_specs=pl.BlockSpec((tm, tn), lambda j, i: (i, j)),
        out_shape=jax.ShapeDtypeStruct((T, K), BF16),
        scratch_shapes=[pltpu.VMEM((K, tn), BF16)],
        compiler_params=_cparams("parallel", "arbitrary"),
        name="qkv_proj",
    )(x, w, tables)


def _attn_bias():
    i = np.arange(ATTN_BQ)[:, None]
    j = np.arange(ATTN_KW)[None, :]
    band = lambda off: np.abs(j - i - off) <= ATTN_RADIUS
    r = ATTN_RADIUS
    masks = [band(0), band(r), band(r) & (j >= r), band(r) & (j < r + ATTN_BQ), band(2 * r)]
    return np.where(np.stack(masks), 0.0, NEG_INF).astype(np.float32)


def _attn_group(q_ref, k_ref, v_ref, bias_ref, o_ref, state_in, state_out, *, d, d_next):
    S = q_ref.shape[0]
    L = S // d
    assert L >= ATTN_KW and L % ATTN_BQ == 0
    log_l = int(math.log2(L))

    def next_order_rows(p0):
        if d == d_next:
            return pl.ds(p0, ATTN_BQ)
        r, u0 = p0 >> log_l, p0 & (L - 1)
        step = d // d_next
        start = (r & (d_next - 1)) * (S // d_next) + step * u0 + r // d_next
        return pl.ds(start, ATTN_BQ, stride=step)

    def one_block(blk):
        p0 = pl.multiple_of(blk * ATTN_BQ, ATTN_BQ)
        k0 = pl.multiple_of(jnp.clip(p0 - ATTN_RADIUS, 0, S - ATTN_KW), ATTN_RADIUS)
        at_start = ((p0 & (L - 1)) == 0).astype(jnp.int32)
        at_end = (((p0 + ATTN_BQ) & (L - 1)) == 0).astype(jnp.int32)
        geom = jnp.where(p0 == 0, 0, jnp.where(p0 == S - ATTN_BQ, 4, 1 + at_start + 2 * at_end))
        s = lax.dot_general(q_ref[pl.ds(p0, ATTN_BQ), :], k_ref[pl.ds(k0, ATTN_KW), :], NT_DIMS,
                            preferred_element_type=F32)
        s = s + bias_ref[geom]
        m = jnp.max(s, axis=-1, keepdims=True)
        p = jnp.exp2(s - m)
        l = jnp.sum(p, axis=-1, keepdims=True)
        acc = jnp.dot(p.astype(BF16), v_ref[pl.ds(k0, ATTN_KW), :], preferred_element_type=F32)
        m = jnp.broadcast_to(m, acc.shape)
        l = jnp.broadcast_to(l, acc.shape)
        if state_in is not None:
            m_ref, l_ref, acc_ref = state_in
            own = pl.ds(p0, ATTN_BQ)
            m_old = m_ref[own, :]
            m_new = jnp.maximum(m_old, m)
            a_old = jnp.exp2(m_old - m_new)
            a_new = jnp.exp2(m - m_new)
            l = a_old * l_ref[own, :] + a_new * l
            acc = a_old * acc_ref[own, :] + a_new * acc
            m = m_new
        dst = next_order_rows(p0)
        if state_out is None:
            o_ref[dst, :] = (acc / l).astype(o_ref.dtype)
        else:
            m_ref, l_ref, acc_ref = state_out
            m_ref[dst, :] = m
            l_ref[dst, :] = l
            acc_ref[dst, :] = acc

    unroll = ATTN_UNROLL * (2 if state_in is None else 1)

    def block_body(i, carry):
        for u in range(unroll):
            one_block(i * unroll + u)
        return carry

    lax.fori_loop(0, S // (ATTN_BQ * unroll), block_body, 0)


def _attn_kernel(*refs):
    n_groups = len(DILATED_GROUPS)
    qkv_refs = refs[:3 * n_groups]
    bias_ref, o_ref = refs[3 * n_groups:3 * n_groups + 2]
    states = [refs[3 * n_groups + 2:][3 * i:3 * i + 3] for i in range(2)]
    assert DILATED_GROUPS[0][1] == 1
    order = list(reversed(range(n_groups)))
    for step, gi in enumerate(order):
        window, dilation = DILATED_GROUPS[gi]
        assert window // (2 * dilation) == ATTN_RADIUS
        last = step == n_groups - 1
        q_ref, k_ref, v_ref = qkv_refs[3 * gi:3 * gi + 3]
        _attn_group(q_ref, k_ref, v_ref, bias_ref, o_ref,
                    None if step == 0 else states[(step - 1) % 2],
                    None if last else states[step % 2],
                    d=dilation, d_next=1 if last else DILATED_GROUPS[order[step + 1]][1])


def _attn_call(qkv, B, S, D):
    H = D // HEAD
    head = pl.BlockSpec((S, HEAD), lambda b, h: (b, h))
    return pl.pallas_call(
        _attn_kernel,
        grid=(B, H),
        in_specs=[head] * len(qkv) + [pl.BlockSpec((5, ATTN_BQ, ATTN_KW), lambda b, h: (0, 0, 0))],
        out_specs=head,
        out_shape=jax.ShapeDtypeStruct((B * S, D), BF16),
        scratch_shapes=[pltpu.VMEM((S, HEAD), F32)] * 6,
        compiler_params=_cparams("parallel", "parallel"),
        name="dilated_attention",
    )(*qkv, jnp.asarray(_attn_bias()))


def kernel(x, norm_w, w_in_hgrn, hgrn_lower_bounds, hgrn_gnorm, w_in_attn, w_out, w_ffn_in, w_ffn_out):
    B, S, D = x.shape
    depth = norm_w.shape[0]
    T = B * S
    xf = x.reshape(T, D).astype(F32)
    lb = _lb_call(hgrn_lower_bounds).reshape(2 * depth, 1, D)
    tables = _rope_tables(S)

    hn = _norm_call(xf, w_pre=norm_w[0, 0])
    for layer in range(depth):
        slot = layer // 2
        if layer % 2 == 0:
            proj = _matmul(hn, w_in_hgrn, slot, tm=IN_PROJ_TILE[0], tn=IN_PROJ_TILE[1], out_dtype=F32)
            mixed = _hgrn_call(proj, lb, layer, depth, hgrn_gnorm[slot], B, S, D)
        else:
            qkv = []
            for gi in range(len(DILATED_GROUPS)):
                for kind in range(3):
                    tabs = None if kind == 2 else tables[gi, 3 * kind:3 * kind + 3]
                    qkv.append(_qkv_call(hn_by_group[gi], w_in_attn, slot, (gi * 3 + kind) * D, tabs, S,
                                         tm=IN_PROJ_TILE[0], tn=IN_PROJ_TILE[1]))
            mixed = _attn_call(qkv, B, S, D)
        xf, hn = _outproj_merge(mixed, w_out, layer, xf, norm_w[layer, 1], norm_w[layer, 2], tm=OUT_PROJ_ROWS)
        act = _matmul_swiglu(hn, w_ffn_in, layer, tm=SWIGLU_TILE[0], tn=SWIGLU_TILE[1])
        h = _matmul(act, w_ffn_out, layer, tm=FFN_OUT_TILE[0], tn=FFN_OUT_TILE[1], out_dtype=BF16)
        if layer + 1 < depth:
            dil = tuple(d for _, d in DILATED_GROUPS if d > 1) if (layer + 1) % 2 == 1 else ()
            xf, hn, *hn_perm = _norm_call(xf, h, norm_w[layer, 3], norm_w[layer + 1, 0], dilations=dil, seq=S)
            hn_by_group = [hn] + hn_perm
        else:
            xf = _norm_call(xf, h, norm_w[layer, 3])
    return xf.reshape(B, S, D).astype(x.dtype)
```

```python
import functools
import math

import numpy as np
import jax
import jax.numpy as jnp
from jax import lax
from jax.experimental import pallas as pl
from jax.experimental.pallas import tpu as pltpu

F32 = jnp.float32
BF16 = jnp.bfloat16

EPS = 1e-6
HEAD = 128
DILATED_GROUPS = ((128, 1), (512, 4), (2048, 16))
ROPE_THETA = 500000.0
ROPE_DIM = HEAD // 4
NEG_INF = -1e30
LOG2E = math.log2(math.e)

GLA_CHUNK = 128
ATTN_BQ = 128
ATTN_RADIUS = 64
ATTN_KW = ATTN_BQ + 2 * ATTN_RADIUS
ATTN_UNROLL = 32

VMEM_LIMIT = 56 * 1024 * 1024
MXU_N = 256

IN_PROJ_TILE = (1024, 1024)
SWIGLU_TILE = (1024, 512)
FFN_OUT_TILE = (512, 512)
OUT_PROJ_ROWS = 512

NT_DIMS = (((1,), (1,)), ((), ()))
TN_DIMS = (((0,), (0,)), ((), ()))


def _cparams(*sem):
    return pltpu.CompilerParams(dimension_semantics=sem, vmem_limit_bytes=VMEM_LIMIT)


def _rms(x, w):
    return x * lax.rsqrt(jnp.mean(x * x, axis=-1, keepdims=True) + EPS) * w


def _sigmoid(x):
    return 1.0 / (1.0 + jnp.exp(-x))


def _norm_kernel(*refs, has_h, has_pre, dilations):
    refs = list(refs)
    x_ref = refs.pop(0)
    x = x_ref[...]
    if has_h:
        h_ref = refs.pop(0)
        wpost_ref = refs.pop(0)
    if has_pre:
        wpre_ref = refs.pop(0)
    if has_h:
        xo_ref = refs.pop(0)
        x = x + _rms(h_ref[...].astype(F32), wpost_ref[...])
        xo_ref[...] = x
    if has_pre:
        hn_ref = refs.pop(0)
        hn = _rms(x, wpre_ref[...])
        hn_ref[...] = hn.astype(BF16)
        if dilations:
            slab_ref = refs[-1]
            rows = hn.shape[0]
            n_slab = hn.shape[1] // HEAD
            for c in range(n_slab):
                slab_ref[c] = hn[:, c * HEAD:(c + 1) * HEAD]
            for d, hp_ref in zip(dilations, refs[:len(dilations)]):
                for c in range(n_slab):
                    for r in range(d):
                        hp_ref[r, :, c * HEAD:(c + 1) * HEAD] = (
                            slab_ref[c, pl.ds(r, rows // d, stride=d), :].astype(BF16))


def _norm_call(x, h=None, w_post=None, w_pre=None, rows=256, dilations=(), seq=None):
    T, D = x.shape
    has_h, has_pre = h is not None, w_pre is not None
    row_spec = pl.BlockSpec((rows, D), lambda i: (i, 0))
    w_spec = pl.BlockSpec((1, D), lambda i: (0, 0))
    args, in_specs, out_shape, out_specs, scratch = [x], [row_spec], [], [], []
    if has_h:
        args += [h, w_post.reshape(1, D)]
        in_specs += [row_spec, w_spec]
        out_shape.append(jax.ShapeDtypeStruct((T, D), F32))
        out_specs.append(row_spec)
    if has_pre:
        args.append(w_pre.reshape(1, D))
        in_specs.append(w_spec)
        out_shape.append(jax.ShapeDtypeStruct((T, D), BF16))
        out_specs.append(row_spec)
    if dilations:
        blocks_per_seq = seq // rows
        for d in dilations:
            out_shape.append(jax.ShapeDtypeStruct((T // seq, d, seq // d, D), BF16))
            out_specs.append(pl.BlockSpec((None, d, rows // d, D),
                                          lambda i: (i // blocks_per_seq, 0, i % blocks_per_seq, 0)))
        scratch.append(pltpu.VMEM((D // HEAD, rows, HEAD), F32))
    outs = pl.pallas_call(
        functools.partial(_norm_kernel, has_h=has_h, has_pre=has_pre, dilations=tuple(dilations)),
        grid=(T // rows,),
        in_specs=in_specs,
        out_specs=out_specs,
        out_shape=out_shape,
        scratch_shapes=scratch,
        compiler_params=_cparams("parallel"),
        name="norm_merge",
    )(*args)
    outs = [o.reshape(T, D) for o in outs]
    return outs if len(outs) > 1 else outs[0]


def _mm_kernel(x_ref, w_ref, o_ref, wb_ref):
    @pl.when(pl.program_id(1) == 0)
    def _():
        wb_ref[...] = w_ref[...].astype(BF16)

    o_ref[...] = jnp.dot(x_ref[...], wb_ref[...], preferred_element_type=F32).astype(o_ref.dtype)


def _matmul(x, w, layer, *, tm, tn, out_dtype):
    T, K = x.shape
    N = w.shape[2]
    return pl.pallas_call(
        _mm_kernel,
        grid=(N // tn, T // tm),
        in_specs=[
            pl.BlockSpec((tm, K), lambda j, i: (i, 0)),
            pl.BlockSpec((None, K, tn), lambda j, i: (layer, 0, j)),
        ],
        out_specs=pl.BlockSpec((tm, tn), lambda j, i: (i, j)),
        out_shape=jax.ShapeDtypeStruct((T, N), out_dtype),
        scratch_shapes=[pltpu.VMEM((K, tn), BF16)],
        compiler_params=_cparams("parallel", "arbitrary"),
        name="matmul",
    )(x, w)


def _swiglu_kernel(x_ref, wg_ref, wu_ref, o_ref, wgb_ref, wub_ref):
    @pl.when(pl.program_id(1) == 0)
    def _():
        wgb_ref[...] = wg_ref[...].astype(BF16)
        wub_ref[...] = wu_ref[...].astype(BF16)

    x = x_ref[...]
    gate = jnp.dot(x, wgb_ref[...], preferred_element_type=F32)
    up = jnp.dot(x, wub_ref[...], preferred_element_type=F32)
    o_ref[...] = (gate * _sigmoid(gate) * up).astype(o_ref.dtype)


def _matmul_swiglu(x, w, layer, *, tm, tn):
    T, K = x.shape
    F = w.shape[2] // 2
    nf = F // tn
    return pl.pallas_call(
        _swiglu_kernel,
        grid=(nf, T // tm),
        in_specs=[
            pl.BlockSpec((tm, K), lambda j, i: (i, 0)),
            pl.BlockSpec((None, K, tn), lambda j, i: (layer, 0, j)),
            pl.BlockSpec((None, K, tn), lambda j, i: (layer, 0, j + nf)),
        ],
        out_specs=pl.BlockSpec((tm, tn), lambda j, i: (i, j)),
        out_shape=jax.ShapeDtypeStruct((T, F), BF16),
        scratch_shapes=[pltpu.VMEM((K, tn), BF16), pltpu.VMEM((K, tn), BF16)],
        compiler_params=_cparams("parallel", "arbitrary"),
        name="matmul_swiglu",
    )(x, w, w)


def _outproj_kernel(a_ref, w_ref, x_ref, wpost_ref, wpre_ref, xo_ref, hn_ref, wb_ref):
    @pl.when(pl.program_id(0) == 0)
    def _():
        wb_ref[...] = w_ref[...].astype(BF16)

    h = jnp.dot(a_ref[...], wb_ref[...], preferred_element_type=F32)
    x = x_ref[...] + _rms(h, wpost_ref[...])
    xo_ref[...] = x
    hn_ref[...] = _rms(x, wpre_ref[...]).astype(BF16)


def _outproj_merge(a, w, layer, x, w_post, w_pre, *, tm):
    T, D = x.shape
    row = lambda dt: pl.BlockSpec((tm, D), lambda i: (i, 0))
    vec = pl.BlockSpec((1, D), lambda i: (0, 0))
    return pl.pallas_call(
        _outproj_kernel,
        grid=(T // tm,),
        in_specs=[row(BF16),
                  pl.BlockSpec((None, D, D), lambda i: (layer, 0, 0), pipeline_mode=pl.Buffered(1)),
                  row(F32), vec, vec],
        out_specs=[row(F32), row(BF16)],
        out_shape=[jax.ShapeDtypeStruct((T, D), F32), jax.ShapeDtypeStruct((T, D), BF16)],
        scratch_shapes=[pltpu.VMEM((D, D), BF16)],
        compiler_params=_cparams("arbitrary"),
        name="outproj_merge",
    )(a, w, x, w_post.reshape(1, D), w_pre.reshape(1, D))


def _lb_kernel(x_ref, o_ref, *, depth):
    for d in range(2):
        r = [x_ref[pl.ds(d * depth + i, 1), :] for i in range(depth)]
        m = functools.reduce(jnp.maximum, r)
        e = [jnp.exp(v - m) for v in r]
        tot = functools.reduce(lambda a, b: a + b, e)
        c = None
        first = None
        for i in range(depth):
            p = e[i] / tot
            c = p if c is None else c + p
            if first is None:
                first = c
            o_ref[pl.ds(d * depth + i, 1), :] = c - first


def _lb_call(lower_bounds):
    two, depth, D = lower_bounds.shape
    return pl.pallas_call(
        functools.partial(_lb_kernel, depth=depth),
        out_shape=jax.ShapeDtypeStruct((two * depth, D), F32),
        name="hgrn_lower_bounds",
    )(lower_bounds.reshape(two * depth, D).astype(F32))


def _gla_level_ids(C, rev):
    t = np.arange(C)[:, None]
    s = np.arange(C)[None, :]
    x = t ^ s
    lv = np.where(x > 0, np.floor(np.log2(np.maximum(x, 1))).astype(np.int64), int(math.log2(C)))
    allowed = (t <= s) if rev else (t >= s)
    return np.where(allowed, lv, -1).astype(np.int32)


def _gla_tri(C, rev):
    t = np.arange(C)[:, None]
    j = np.arange(C)[None, :]
    return ((j >= t) if rev else (j <= t)).astype(np.float32)


def _split_row_exponent(b, w, rev):
    C = b.shape[0]
    span = 2 * w
    mid = w if rev else w - 1
    pieces = []
    for a in range(C // span):
        lo, split, hi = a * span, a * span + w, (a + 1) * span
        row = b[lo + mid:lo + mid + 1, :]
        if w % 8 == 0:
            first, second = b[lo:split, :], b[split:hi, :]
            pieces += [first - row, row - second] if rev else [row - first, second - row]
        else:
            pieces.append(-jnp.abs(b[lo:hi, :] - row))
    return pieces[0] if len(pieces) == 1 else jnp.concatenate(pieces, axis=0)


def _fine_decay(fg, w, rev):
    C = fg.shape[0]
    t4 = lax.broadcasted_iota(jnp.int32, fg.shape, 0) & 3
    if w == 1:
        keep = (t4 & 1) == (0 if rev else 1)
        return jnp.where(keep, fg, 1.0)
    nxt = pltpu.roll(fg, C - 1, 0)
    prv = pltpu.roll(fg, 1, 0)
    by_row = [fg * nxt, fg, 1.0, prv] if rev else [nxt, 1.0, fg, fg * prv]
    return jnp.where(t4 == 0, by_row[0], jnp.where(t4 == 1, by_row[1], jnp.where(t4 == 2, by_row[2], by_row[3])))


def _gla_chunk(qb, f_raw, vb, lb, st_ref, lv_ref, tri_ref, rev):
    C = qb.shape[0]
    nlev = int(math.log2(C))
    fg = lb + (1.0 - lb) * _sigmoid(f_raw)
    kb = (1.0 - fg).astype(BF16)
    lg = jnp.log(fg) * LOG2E
    hi = lg.astype(BF16)
    lo = (lg - hi.astype(F32)).astype(BF16)
    tri = tri_ref[...]
    b = jnp.dot(tri, hi, preferred_element_type=F32) + jnp.dot(tri, lo, preferred_element_type=F32)

    lv = lv_ref[...]
    scores = jnp.where(lv == nlev, lax.dot_general(qb, kb, NT_DIMS, preferred_element_type=F32), 0.0)
    for p in range(nlev):
        w = 1 << p
        if w < 4:
            z = _fine_decay(fg, w, rev).astype(BF16)
        else:
            z = jnp.exp2(_split_row_exponent(b, w, rev)).astype(BF16)
        s = lax.dot_general(qb * z, kb * z, NT_DIMS, preferred_element_type=F32)
        scores = jnp.where(lv == p, s, scores)

    b_edge = b[0:1, :] if rev else b[C - 1:C, :]
    q_st = qb * jnp.exp2(b).astype(BF16)
    k_st = kb * jnp.exp2(b_edge - b).astype(BF16)
    st = st_ref[...]
    o = (jnp.dot(scores.astype(BF16), vb, preferred_element_type=F32)
         + lax.dot_general(q_st, st.astype(BF16), NT_DIMS, preferred_element_type=F32))
    st_ref[...] = st * jnp.exp2(b_edge) + lax.dot_general(vb, k_st, TN_DIMS, preferred_element_type=F32)
    return o


def _hgrn_kernel(q_ref, ff_ref, fb_ref, i_ref, g_ref, lbf_ref, lbb_ref, gn_ref,
                 lvf_ref, lvb_ref, trif_ref, trib_ref, o_ref,
                 qs_ref, vs_ref, of_ref, ob_ref, stf_ref, stb_ref, *, chunk):
    S = q_ref.shape[0]
    n_chunks = S // chunk

    def chunk_rows(c):
        return pl.ds(pl.multiple_of(c * chunk, chunk), chunk)

    def stage_body(c, carry):
        rows = chunk_rows(c)
        qr = q_ref[rows, :]
        qs_ref[rows, :] = (qr * _sigmoid(qr)).astype(BF16)
        vs_ref[rows, :] = i_ref[rows, :].astype(BF16)
        return carry

    lax.fori_loop(0, n_chunks, stage_body, 0, unroll=4)
    stf_ref[...] = jnp.zeros_like(stf_ref)
    stb_ref[...] = jnp.zeros_like(stb_ref)

    def scan_body(j, carry):
        rf = chunk_rows(j)
        rb = chunk_rows(n_chunks - 1 - j)
        of_ref[rf, :] = _gla_chunk(qs_ref[rf, :], ff_ref[rf, :], vs_ref[rf, :], lbf_ref[...],
                                   stf_ref, lvf_ref, trif_ref, False)
        ob_ref[rb, :] = _gla_chunk(qs_ref[rb, :], fb_ref[rb, :], vs_ref[rb, :], lbb_ref[...],
                                   stb_ref, lvb_ref, trib_ref, True)
        return carry

    lax.fori_loop(0, n_chunks, scan_body, 0, unroll=8)

    def out_body(c, carry):
        rows = chunk_rows(c)
        o = of_ref[rows, :] + ob_ref[rows, :]
        gate = g_ref[rows, :]
        o_ref[rows, :] = (_rms(o, gn_ref[...]) * (gate * _sigmoid(gate))).astype(o_ref.dtype)
        return carry

    lax.fori_loop(0, n_chunks, out_body, 0, unroll=4)


def _hgrn_call(proj, lb, layer, depth, gnorm, B, S, D):
    H = D // HEAD
    C = GLA_CHUNK
    col = lambda c: pl.BlockSpec((S, HEAD), lambda b, h, c=c: (b, c * H + h))
    lb_spec = lambda r: pl.BlockSpec((None, 1, HEAD), lambda b, h, r=r: (r, 0, h))
    const = lambda shape: pl.BlockSpec(shape, lambda b, h: (0, 0))
    return pl.pallas_call(
        functools.partial(_hgrn_kernel, chunk=C),
        grid=(B, H),
        in_specs=[col(0), col(1), col(2), col(3), col(4),
                  lb_spec(layer), lb_spec(depth + layer), const((1, HEAD)),
                  const((C, C)), const((C, C)), const((C, C)), const((C, C))],
        out_specs=pl.BlockSpec((S, HEAD), lambda b, h: (b, h)),
        out_shape=jax.ShapeDtypeStruct((B * S, D), BF16),
        scratch_shapes=[pltpu.VMEM((S, HEAD), BF16), pltpu.VMEM((S, HEAD), BF16),
                        pltpu.VMEM((S, HEAD), F32), pltpu.VMEM((S, HEAD), F32),
                        pltpu.VMEM((HEAD, HEAD), F32), pltpu.VMEM((HEAD, HEAD), F32)],
        compiler_params=_cparams("parallel", "parallel"),
        name="hgrn2_mixer",
    )(proj, proj, proj, proj, proj, lb, lb, gnorm.reshape(1, HEAD),
      jnp.asarray(_gla_level_ids(C, False)), jnp.asarray(_gla_level_ids(C, True)),
      jnp.asarray(_gla_tri(C, False), BF16), jnp.asarray(_gla_tri(C, True), BF16))


def _rope_tables(S):
    half = ROPE_DIM // 2
    pos = jnp.arange(S, dtype=F32)
    inv_freq = ROPE_THETA ** (-(jnp.arange(0, ROPE_DIM, 2, dtype=F32) / ROPE_DIM))
    ang = pos[:, None] * inv_freq[None, :]
    ang = jnp.concatenate([ang, ang], axis=-1)
    cos, sin = jnp.cos(ang), jnp.sin(ang)
    pad = lambda t, val: jnp.concatenate([t, jnp.full((S, HEAD - t.shape[1]), val, F32)], axis=-1)
    zeros = jnp.zeros((S, half), F32)
    k_tabs = [pad(cos, 1.0),
              pad(jnp.concatenate([-sin[:, :half], zeros], axis=-1), 0.0),
              pad(jnp.concatenate([zeros, sin[:, half:]], axis=-1), 0.0)]
    qs = HEAD ** -0.5 * LOG2E
    tabs = jnp.stack([t * qs for t in k_tabs] + k_tabs)
    return jnp.stack([tabs.reshape(6, S // d, d, HEAD).transpose(0, 2, 1, 3).reshape(6, S, HEAD)
                      for _, d in DILATED_GROUPS])


def _qkv_kernel(x_ref, w_ref, tab_ref, o_ref, wb_ref, *, rope):
    @pl.when(pl.program_id(1) == 0)
    def _():
        wb_ref[...] = w_ref[...].astype(BF16)

    tn = wb_ref.shape[1]
    half = ROPE_DIM // 2
    x = x_ref[...]
    for nt in range(tn // MXU_N):
        res = jnp.dot(x, wb_ref[:, nt * MXU_N:(nt + 1) * MXU_N], preferred_element_type=F32)
        for s in range(MXU_N // HEAD):
            y = res[:, s * HEAD:(s + 1) * HEAD]
            if rope:
                y = (y * tab_ref[0] + pltpu.roll(y, HEAD - half, 1) * tab_ref[1]
                     + pltpu.roll(y, half, 1) * tab_ref[2])
            lane0 = nt * MXU_N + s * HEAD
            o_ref[:, lane0:lane0 + HEAD] = y.astype(BF16)


def _qkv_call(x, w, layer, col0, tables, S, *, tm, tn):
    T, K = x.shape
    rope = tables is not None
    if not rope:
        tables = jnp.zeros((3, tm, HEAD), F32)
    blocks_per_seq = S // tm
    return pl.pallas_call(
        functools.partial(_qkv_kernel, rope=rope),
        grid=(K // tn, T // tm),
        in_specs=[
            pl.BlockSpec((tm, K), lambda j, i: (i, 0)),
            pl.BlockSpec((None, K, tn), lambda j, i: (layer, 0, col0 // tn + j)),
            pl.BlockSpec((3, tm, HEAD), (lambda j, i: (0, i % blocks_per_seq, 0)) if rope else (lambda j, i: (0, 0, 0))),
        ],
        out_specs=pl.BlockSpec((tm, tn), lambda j, i: (i, j)),
        out_shape=jax.ShapeDtypeStruct((T, K), BF16),
        scratch_shapes=[pltpu.VMEM((K, tn), BF16)],
        compiler_params=_cparams("parallel", "arbitrary"),
        name="qkv_proj",
    )(x, w, tables)


def _attn_bias():
    i = np.arange(ATTN_BQ)[:, None]
    j = np.arange(ATTN_KW)[None, :]
    band = lambda off: np.abs(j - i - off) <= ATTN_RADIUS
    r = ATTN_RADIUS
    masks = [band(0), band(r), band(r) & (j >= r), band(r) & (j < r + ATTN_BQ), band(2 * r)]
    return np.where(np.stack(masks), 0.0, NEG_INF).astype(np.float32)


def _attn_group(q_ref, k_ref, v_ref, bias_ref, o_ref, state_in, state_out, *, d, d_next):
    S = q_ref.shape[0]
    L = S // d
    assert L >= ATTN_KW and L % ATTN_BQ == 0
    log_l = int(math.log2(L))

    def next_order_rows(p0):
        if d == d_next:
            return pl.ds(p0, ATTN_BQ)
        r, u0 = p0 >> log_l, p0 & (L - 1)
        step = d // d_next
        start = (r & (d_next - 1)) * (S // d_next) + step * u0 + r // d_next
        return pl.ds(start, ATTN_BQ, stride=step)

    def one_block(blk):
        p0 = pl.multiple_of(blk * ATTN_BQ, ATTN_BQ)
        k0 = pl.multiple_of(jnp.clip(p0 - ATTN_RADIUS, 0, S - ATTN_KW), ATTN_RADIUS)
        at_start = ((p0 & (L - 1)) == 0).astype(jnp.int32)
        at_end = (((p0 + ATTN_BQ) & (L - 1)) == 0).astype(jnp.int32)
        geom = jnp.where(p0 == 0, 0, jnp.where(p0 == S - ATTN_BQ, 4, 1 + at_start + 2 * at_end))
        s = lax.dot_general(q_ref[pl.ds(p0, ATTN_BQ), :], k_ref[pl.ds(k0, ATTN_KW), :], NT_DIMS,
                            preferred_element_type=F32)
        s = s + bias_ref[geom]
        m = jnp.max(s, axis=-1, keepdims=True)
        p = jnp.exp2(s - m)
        l = jnp.sum(p, axis=-1, keepdims=True)
        acc = jnp.dot(p.astype(BF16), v_ref[pl.ds(k0, ATTN_KW), :], preferred_element_type=F32)
        m = jnp.broadcast_to(m, acc.shape)
        l = jnp.broadcast_to(l, acc.shape)
        if state_in is not None:
            m_ref, l_ref, acc_ref = state_in
            own = pl.ds(p0, ATTN_BQ)
            m_old = m_ref[own, :]
            m_new = jnp.maximum(m_old, m)
            a_old = jnp.exp2(m_old - m_new)
            a_new = jnp.exp2(m - m_new)
            l = a_old * l_ref[own, :] + a_new * l
            acc = a_old * acc_ref[own, :] + a_new * acc
            m = m_new
        dst = next_order_rows(p0)
        if state_out is None:
            o_ref[dst, :] = (acc / l).astype(o_ref.dtype)
        else:
            m_ref, l_ref, acc_ref = state_out
            m_ref[dst, :] = m
            l_ref[dst, :] = l
            acc_ref[dst, :] = acc

    unroll = ATTN_UNROLL

    def block_body(i, carry):
        for u in range(unroll):
            one_block(i * unroll + u)
        return carry

    lax.fori_loop(0, S // (ATTN_BQ * unroll), block_body, 0)


def _attn_kernel(*refs):
    n_groups = len(DILATED_GROUPS)
    qkv_refs = refs[:3 * n_groups]
    bias_ref, o_ref = refs[3 * n_groups:3 * n_groups + 2]
    states = [refs[3 * n_groups + 2:][3 * i:3 * i + 3] for i in range(2)]
    assert DILATED_GROUPS[0][1] == 1
    order = list(reversed(range(n_groups)))
    for step, gi in enumerate(order):
        window, dilation = DILATED_GROUPS[gi]
        assert window // (2 * dilation) == ATTN_RADIUS
        last = step == n_groups - 1
        q_ref, k_ref, v_ref = qkv_refs[3 * gi:3 * gi + 3]
        _attn_group(q_ref, k_ref, v_ref, bias_ref, o_ref,
                    None if step == 0 else states[(step - 1) % 2],
                    None if last else states[step % 2],
                    d=dilation, d_next=1 if last else DILATED_GROUPS[order[step + 1]][1])


def _attn_call(qkv, B, S, D):
    H = D // HEAD
    head = pl.BlockSpec((S, HEAD), lambda b, h: (b, h))
    return pl.pallas_call(
        _attn_kernel,
        grid=(B, H),
        in_specs=[head] * len(qkv) + [pl.BlockSpec((5, ATTN_BQ, ATTN_KW), lambda b, h: (0, 0, 0))],
        out_specs=head,
        out_shape=jax.ShapeDtypeStruct((B * S, D), BF16),
        scratch_shapes=[pltpu.VMEM((S, HEAD), F32)] * 6,
        compiler_params=_cparams("parallel", "parallel"),
        name="dilated_attention",
    )(*qkv, jnp.asarray(_attn_bias()))


def kernel(x, norm_w, w_in_hgrn, hgrn_lower_bounds, hgrn_gnorm, w_in_attn, w_out, w_ffn_in, w_ffn_out):
    B, S, D = x.shape
    depth = norm_w.shape[0]
    T = B * S
    xf = x.reshape(T, D).astype(F32)
    lb = _lb_call(hgrn_lower_bounds).reshape(2 * depth, 1, D)
    tables = _rope_tables(S)

    hn = _norm_call(xf, w_pre=norm_w[0, 0])
    for layer in range(depth):
        slot = layer // 2
        if layer % 2 == 0:
            proj = _matmul(hn, w_in_hgrn, slot, tm=IN_PROJ_TILE[0], tn=IN_PROJ_TILE[1], out_dtype=F32)
            mixed = _hgrn_call(proj, lb, layer, depth, hgrn_gnorm[slot], B, S, D)
        else:
            qkv = []
            for gi in range(len(DILATED_GROUPS)):
                for kind in range(3):
                    tabs = None if kind == 2 else tables[gi, 3 * kind:3 * kind + 3]
                    qkv.append(_qkv_call(hn_by_group[gi], w_in_attn, slot, (gi * 3 + kind) * D, tabs, S,
                                         tm=IN_PROJ_TILE[0], tn=IN_PROJ_TILE[1]))
            mixed = _attn_call(qkv, B, S, D)
        xf, hn = _outproj_merge(mixed, w_out, layer, xf, norm_w[layer, 1], norm_w[layer, 2], tm=OUT_PROJ_ROWS)
        act = _matmul_swiglu(hn, w_ffn_in, layer, tm=SWIGLU_TILE[0], tn=SWIGLU_TILE[1])
        h = _matmul(act, w_ffn_out, layer, tm=FFN_OUT_TILE[0], tn=FFN_OUT_TILE[1], out_dtype=BF16)
        if layer + 1 < depth:
            dil = tuple(d for _, d in DILATED_GROUPS if d > 1) if (layer + 1) % 2 == 1 else ()
            xf, hn, *hn_perm = _norm_call(xf, h, norm_w[layer, 3], norm_w[layer + 1, 0], dilations=dil, seq=S)
            hn_by_group = [hn] + hn_perm
        else:
            xf = _norm_call(xf, h, norm_w[layer, 3])
    return xf.reshape(B, S, D).astype(x.dtype)
```

```python
import functools
import math

import numpy as np
import jax
import jax.numpy as jnp
from jax import lax
from jax.experimental import pallas as pl
from jax.experimental.pallas import tpu as pltpu

F32 = jnp.float32
BF16 = jnp.bfloat16

EPS = 1e-6
HEAD = 128
DILATED_GROUPS = ((128, 1), (512, 4), (2048, 16))
ROPE_THETA = 500000.0
ROPE_DIM = HEAD // 4
NEG_INF = -1e30
LOG2E = math.log2(math.e)

GLA_CHUNK = 128
ATTN_BQ = 128
ATTN_RADIUS = 64
ATTN_KW = ATTN_BQ + 2 * ATTN_RADIUS
ATTN_UNROLL = 32

VMEM_LIMIT = 56 * 1024 * 1024
MXU_N = 256

IN_PROJ_TILE = (1024, 1024)
SWIGLU_TILE = (1024, 512)
FFN_OUT_TILE = (512, 512)
OUT_PROJ_ROWS = 512

NT_DIMS = (((1,), (1,)), ((), ()))
TN_DIMS = (((0,), (0,)), ((), ()))


def _cparams(*sem):
    return pltpu.CompilerParams(dimension_semantics=sem, vmem_limit_bytes=VMEM_LIMIT)


def _rms(x, w):
    return x * lax.rsqrt(jnp.mean(x * x, axis=-1, keepdims=True) + EPS) * w


def _sigmoid(x):
    return 1.0 / (1.0 + jnp.exp(-x))


def _norm_kernel(*refs, has_h, has_pre, dilations):
    refs = list(refs)
    x_ref = refs.pop(0)
    x = x_ref[...]
    if has_h:
        h_ref = refs.pop(0)
        wpost_ref = refs.pop(0)
    if has_pre:
        wpre_ref = refs.pop(0)
    if has_h:
        xo_ref = refs.pop(0)
        x = x + _rms(h_ref[...].astype(F32), wpost_ref[...])
        xo_ref[...] = x
    if has_pre:
        hn_ref = refs.pop(0)
        hn = _rms(x, wpre_ref[...])
        hn_ref[...] = hn.astype(BF16)
        if dilations:
            slab_ref = refs[-1]
            rows = hn.shape[0]
            n_slab = hn.shape[1] // HEAD
            for c in range(n_slab):
                slab_ref[c] = hn[:, c * HEAD:(c + 1) * HEAD]
            for d, hp_ref in zip(dilations, refs[:len(dilations)]):
                for c in range(n_slab):
                    for r in range(d):
                        hp_ref[r, :, c * HEAD:(c + 1) * HEAD] = (
                            slab_ref[c, pl.ds(r, rows // d, stride=d), :].astype(BF16))


def _norm_call(x, h=None, w_post=None, w_pre=None, rows=256, dilations=(), seq=None):
    T, D = x.shape
    has_h, has_pre = h is not None, w_pre is not None
    row_spec = pl.BlockSpec((rows, D), lambda i: (i, 0))
    w_spec = pl.BlockSpec((1, D), lambda i: (0, 0))
    args, in_specs, out_shape, out_specs, scratch = [x], [row_spec], [], [], []
    if has_h:
        args += [h, w_post.reshape(1, D)]
        in_specs += [row_spec, w_spec]
        out_shape.append(jax.ShapeDtypeStruct((T, D), F32))
        out_specs.append(row_spec)
    if has_pre:
        args.append(w_pre.reshape(1, D))
        in_specs.append(w_spec)
        out_shape.append(jax.ShapeDtypeStruct((T, D), BF16))
        out_specs.append(row_spec)
    if dilations:
        blocks_per_seq = seq // rows
        for d in dilations:
            out_shape.append(jax.ShapeDtypeStruct((T // seq, d, seq // d, D), BF16))
            out_specs.append(pl.BlockSpec((None, d, rows // d, D),
                                          lambda i: (i // blocks_per_seq, 0, i % blocks_per_seq, 0)))
        scratch.append(pltpu.VMEM((D // HEAD, rows, HEAD), F32))
    outs = pl.pallas_call(
        functools.partial(_norm_kernel, has_h=has_h, has_pre=has_pre, dilations=tuple(dilations)),
        grid=(T // rows,),
        in_specs=in_specs,
        out_specs=out_specs,
        out_shape=out_shape,
        scratch_shapes=scratch,
        compiler_params=_cparams("parallel"),
        name="norm_merge",
    )(*args)
    outs = [o.reshape(T, D) for o in outs]
    return outs if len(outs) > 1 else outs[0]


def _mm_kernel(x_ref, w_ref, o_ref, wb_ref):
    @pl.when(pl.program_id(1) == 0)
    def _():
        wb_ref[...] = w_ref[...].astype(BF16)

    o_ref[...] = jnp.dot(x_ref[...], wb_ref[...], preferred_element_type=F32).astype(o_ref.dtype)


def _matmul(x, w, layer, *, tm, tn, out_dtype):
    T, K = x.shape
    N = w.shape[2]
    return pl.pallas_call(
        _mm_kernel,
        grid=(N // tn, T // tm),
        in_specs=[
            pl.BlockSpec((tm, K), lambda j, i: (i, 0)),
            pl.BlockSpec((None, K, tn), lambda j, i: (layer, 0, j)),
        ],
        out_specs=pl.BlockSpec((tm, tn), lambda j, i: (i, j)),
        out_shape=jax.ShapeDtypeStruct((T, N), out_dtype),
        scratch_shapes=[pltpu.VMEM((K, tn), BF16)],
        compiler_params=_cparams("parallel", "arbitrary"),
        name="matmul",
    )(x, w)


def _swiglu_kernel(x_ref, wg_ref, wu_ref, o_ref, wgb_ref, wub_ref):
    @pl.when(pl.program_id(1) == 0)
    def _():
        wgb_ref[...] = wg_ref[...].astype(BF16)
        wub_ref[...] = wu_ref[...].astype(BF16)

    x = x_ref[...]
    gate = jnp.dot(x, wgb_ref[...], preferred_element_type=F32)
    up = jnp.dot(x, wub_ref[...], preferred_element_type=F32)
    o_ref[...] = (gate * _sigmoid(gate) * up).astype(o_ref.dtype)


def _matmul_swiglu(x, w, layer, *, tm, tn):
    T, K = x.shape
    F = w.shape[2] // 2
    nf = F // tn
    return pl.pallas_call(
        _swiglu_kernel,
        grid=(nf, T // tm),
        in_specs=[
            pl.BlockSpec((tm, K), lambda j, i: (i, 0)),
            pl.BlockSpec((None, K, tn), lambda j, i: (layer, 0, j)),
            pl.BlockSpec((None, K, tn), lambda j, i: (layer, 0, j + nf)),
        ],
        out_specs=pl.BlockSpec((tm, tn), lambda j, i: (i, j)),
        out_shape=jax.ShapeDtypeStruct((T, F), BF16),
        scratch_shapes=[pltpu.VMEM((K, tn), BF16), pltpu.VMEM((K, tn), BF16)],
        compiler_params=_cparams("parallel", "arbitrary"),
        name="matmul_swiglu",
    )(x, w, w)


def _outproj_kernel(a_ref, w_ref, x_ref, wpost_ref, wpre_ref, xo_ref, hn_ref, wb_ref):
    @pl.when(pl.program_id(0) == 0)
    def _():
        wb_ref[...] = w_ref[...].astype(BF16)

    h = jnp.dot(a_ref[...], wb_ref[...], preferred_element_type=F32)
    x = x_ref[...] + _rms(h, wpost_ref[...])
    xo_ref[...] = x
    hn_ref[...] = _rms(x, wpre_ref[...]).astype(BF16)


def _outproj_merge(a, w, layer, x, w_post, w_pre, *, tm):
    T, D = x.shape
    row = lambda dt: pl.BlockSpec((tm, D), lambda i: (i, 0))
    vec = pl.BlockSpec((1, D), lambda i: (0, 0))
    return pl.pallas_call(
        _outproj_kernel,
        grid=(T // tm,),
        in_specs=[row(BF16),
                  pl.BlockSpec((None, D, D), lambda i: (layer, 0, 0), pipeline_mode=pl.Buffered(1)),
                  row(F32), vec, vec],
        out_specs=[row(F32), row(BF16)],
        out_shape=[jax.ShapeDtypeStruct((T, D), F32), jax.ShapeDtypeStruct((T, D), BF16)],
        scratch_shapes=[pltpu.VMEM((D, D), BF16)],
        compiler_params=_cparams("arbitrary"),
        name="outproj_merge",
    )(a, w, x, w_post.reshape(1, D), w_pre.reshape(1, D))


def _lb_kernel(x_ref, o_ref, *, depth):
    for d in range(2):
        r = [x_ref[pl.ds(d * depth + i, 1), :] for i in range(depth)]
        m = functools.reduce(jnp.maximum, r)
        e = [jnp.exp(v - m) for v in r]
        tot = functools.reduce(lambda a, b: a + b, e)
        c = None
        first = None
        for i in range(depth):
            p = e[i] / tot
            c = p if c is None else c + p
            if first is None:
                first = c
            o_ref[pl.ds(d * depth + i, 1), :] = c - first


def _lb_call(lower_bounds):
    two, depth, D = lower_bounds.shape
    return pl.pallas_call(
        functools.partial(_lb_kernel, depth=depth),
        out_shape=jax.ShapeDtypeStruct((two * depth, D), F32),
        name="hgrn_lower_bounds",
    )(lower_bounds.reshape(two * depth, D).astype(F32))


def _gla_level_ids(C, rev):
    t = np.arange(C)[:, None]
    s = np.arange(C)[None, :]
    x = t ^ s
    lv = np.where(x > 0, np.floor(np.log2(np.maximum(x, 1))).astype(np.int64), int(math.log2(C)))
    allowed = (t <= s) if rev else (t >= s)
    return np.where(allowed, lv, -1).astype(np.int32)


def _gla_tri(C, rev):
    t = np.arange(C)[:, None]
    j = np.arange(C)[None, :]
    return ((j >= t) if rev else (j <= t)).astype(np.float32)


def _split_row_exponent(b, w, rev):
    C = b.shape[0]
    span = 2 * w
    mid = w if rev else w - 1
    pieces = []
    for a in range(C // span):
        lo, split, hi = a * span, a * span + w, (a + 1) * span
        row = b[lo + mid:lo + mid + 1, :]
        if w % 8 == 0:
            first, second = b[lo:split, :], b[split:hi, :]
            pieces += [first - row, row - second] if rev else [row - first, second - row]
        else:
            pieces.append(-jnp.abs(b[lo:hi, :] - row))
    return pieces[0] if len(pieces) == 1 else jnp.concatenate(pieces, axis=0)


def _fine_decay(fg, w, rev):
    C = fg.shape[0]
    t4 = lax.broadcasted_iota(jnp.int32, fg.shape, 0) & 3
    if w == 1:
        keep = (t4 & 1) == (0 if rev else 1)
        return jnp.where(keep, fg, 1.0)
    nxt = pltpu.roll(fg, C - 1, 0)
    prv = pltpu.roll(fg, 1, 0)
    by_row = [fg * nxt, fg, 1.0, prv] if rev else [nxt, 1.0, fg, fg * prv]
    return jnp.where(t4 == 0, by_row[0], jnp.where(t4 == 1, by_row[1], jnp.where(t4 == 2, by_row[2], by_row[3])))


def _gla_chunk(qb, f_raw, vb, lb, st_ref, lv_ref, tri_ref, rev):
    C = qb.shape[0]
    nlev = int(math.log2(C))
    fg = lb + (1.0 - lb) * _sigmoid(f_raw)
    kb = (1.0 - fg).astype(BF16)
    lg = jnp.log(fg) * LOG2E
    hi = lg.astype(BF16)
    lo = (lg - hi.astype(F32)).astype(BF16)
    tri = tri_ref[...]
    b = jnp.dot(tri, hi, preferred_element_type=F32) + jnp.dot(tri, lo, preferred_element_type=F32)

    lv = lv_ref[...]
    scores = jnp.where(lv == nlev, lax.dot_general(qb, kb, NT_DIMS, preferred_element_type=F32), 0.0)
    for p in range(nlev):
        w = 1 << p
        if w < 4:
            z = _fine_decay(fg, w, rev).astype(BF16)
        else:
            z = jnp.exp2(_split_row_exponent(b, w, rev)).astype(BF16)
        s = lax.dot_general(qb * z, kb * z, NT_DIMS, preferred_element_type=F32)
        scores = jnp.where(lv == p, s, scores)

    b_edge = b[0:1, :] if rev else b[C - 1:C, :]
    q_st = qb * jnp.exp2(b).astype(BF16)
    k_st = kb * jnp.exp2(b_edge - b).astype(BF16)
    st = st_ref[...]
    o = (jnp.dot(scores.astype(BF16), vb, preferred_element_type=F32)
         + lax.dot_general(q_st, st.astype(BF16), NT_DIMS, preferred_element_type=F32))
    st_ref[...] = st * jnp.exp2(b_edge) + lax.dot_general(vb, k_st, TN_DIMS, preferred_element_type=F32)
    return o


def _hgrn_kernel(q_ref, ff_ref, fb_ref, i_ref, g_ref, lbf_ref, lbb_ref, gn_ref,
                 lvf_ref, lvb_ref, trif_ref, trib_ref, o_ref,
                 qs_ref, vs_ref, of_ref, ob_ref, stf_ref, stb_ref, *, chunk):
    S = q_ref.shape[0]
    n_chunks = S // chunk

    def chunk_rows(c):
        return pl.ds(pl.multiple_of(c * chunk, chunk), chunk)

    def stage_body(c, carry):
        rows = chunk_rows(c)
        qr = q_ref[rows, :]
        qs_ref[rows, :] = (qr * _sigmoid(qr)).astype(BF16)
        vs_ref[rows, :] = i_ref[rows, :].astype(BF16)
        return carry

    lax.fori_loop(0, n_chunks, stage_body, 0, unroll=4)
    stf_ref[...] = jnp.zeros_like(stf_ref)
    stb_ref[...] = jnp.zeros_like(stb_ref)

    def scan_body(j, carry):
        rf = chunk_rows(j)
        rb = chunk_rows(n_chunks - 1 - j)
        of_ref[rf, :] = _gla_chunk(qs_ref[rf, :], ff_ref[rf, :], vs_ref[rf, :], lbf_ref[...],
                                   stf_ref, lvf_ref, trif_ref, False)
        ob_ref[rb, :] = _gla_chunk(qs_ref[rb, :], fb_ref[rb, :], vs_ref[rb, :], lbb_ref[...],
                                   stb_ref, lvb_ref, trib_ref, True)
        return carry

    lax.fori_loop(0, n_chunks, scan_body, 0, unroll=16)

    def out_body(c, carry):
        rows = chunk_rows(c)
        o = of_ref[rows, :] + ob_ref[rows, :]
        gate = g_ref[rows, :]
        o_ref[rows, :] = (_rms(o, gn_ref[...]) * (gate * _sigmoid(gate))).astype(o_ref.dtype)
        return carry

    lax.fori_loop(0, n_chunks, out_body, 0, unroll=4)


def _hgrn_call(proj, lb, layer, depth, gnorm, B, S, D):
    H = D // HEAD
    C = GLA_CHUNK
    col = lambda c: pl.BlockSpec((S, HEAD), lambda b, h, c=c: (b, c * H + h))
    lb_spec = lambda r: pl.BlockSpec((None, 1, HEAD), lambda b, h, r=r: (r, 0, h))
    const = lambda shape: pl.BlockSpec(shape, lambda b, h: (0, 0))
    return pl.pallas_call(
        functools.partial(_hgrn_kernel, chunk=C),
        grid=(B, H),
        in_specs=[col(0), col(1), col(2), col(3), col(4),
                  lb_spec(layer), lb_spec(depth + layer), const((1, HEAD)),
                  const((C, C)), const((C, C)), const((C, C)), const((C, C))],
        out_specs=pl.BlockSpec((S, HEAD), lambda b, h: (b, h)),
        out_shape=jax.ShapeDtypeStruct((B * S, D), BF16),
        scratch_shapes=[pltpu.VMEM((S, HEAD), BF16), pltpu.VMEM((S, HEAD), BF16),
                        pltpu.VMEM((S, HEAD), F32), pltpu.VMEM((S, HEAD), F32),
                        pltpu.VMEM((HEAD, HEAD), F32), pltpu.VMEM((HEAD, HEAD), F32)],
        compiler_params=_cparams("parallel", "parallel"),
        name="hgrn2_mixer",
    )(proj, proj, proj, proj, proj, lb, lb, gnorm.reshape(1, HEAD),
      jnp.asarray(_gla_level_ids(C, False)), jnp.asarray(_gla_level_ids(C, True)),
      jnp.asarray(_gla_tri(C, False), BF16), jnp.asarray(_gla_tri(C, True), BF16))


def _rope_tables(S):
    half = ROPE_DIM // 2
    pos = jnp.arange(S, dtype=F32)
    inv_freq = ROPE_THETA ** (-(jnp.arange(0, ROPE_DIM, 2, dtype=F32) / ROPE_DIM))
    ang = pos[:, None] * inv_freq[None, :]
    ang = jnp.concatenate([ang, ang], axis=-1)
    cos, sin = jnp.cos(ang), jnp.sin(ang)
    pad = lambda t, val: jnp.concatenate([t, jnp.full((S, HEAD - t.shape[1]), val, F32)], axis=-1)
    zeros = jnp.zeros((S, half), F32)
    k_tabs = [pad(cos, 1.0),
              pad(jnp.concatenate([-sin[:, :half], zeros], axis=-1), 0.0),
              pad(jnp.concatenate([zeros, sin[:, half:]], axis=-1), 0.0)]
    qs = HEAD ** -0.5 * LOG2E
    tabs = jnp.stack([t * qs for t in k_tabs] + k_tabs)
    return jnp.stack([tabs.reshape(6, S // d, d, HEAD).transpose(0, 2, 1, 3).reshape(6, S, HEAD)
                      for _, d in DILATED_GROUPS])


def _qkv_kernel(x_ref, w_ref, tab_ref, o_ref, wb_ref, *, rope):
    @pl.when(pl.program_id(1) == 0)
    def _():
        wb_ref[...] = w_ref[...].astype(BF16)

    tn = wb_ref.shape[1]
    half = ROPE_DIM // 2
    x = x_ref[...]
    for nt in range(tn // MXU_N):
        res = jnp.dot(x, wb_ref[:, nt * MXU_N:(nt + 1) * MXU_N], preferred_element_type=F32)
        for s in range(MXU_N // HEAD):
            y = res[:, s * HEAD:(s + 1) * HEAD]
            if rope:
                y = (y * tab_ref[0] + pltpu.roll(y, HEAD - half, 1) * tab_ref[1]
                     + pltpu.roll(y, half, 1) * tab_ref[2])
            lane0 = nt * MXU_N + s * HEAD
            o_ref[:, lane0:lane0 + HEAD] = y.astype(BF16)


def _qkv_call(x, w, layer, col0, tables, S, *, tm, tn):
    T, K = x.shape
    rope = tables is not None
    if not rope:
        tables = jnp.zeros((3, tm, HEAD), F32)
    blocks_per_seq = S // tm
    return pl.pallas_call(
        functools.partial(_qkv_kernel, rope=rope),
        grid=(K // tn, T // tm),
        in_specs=[
            pl.BlockSpec((tm, K), lambda j, i: (i, 0)),
            pl.BlockSpec((None, K, tn), lambda j, i: (layer, 0, col0 // tn + j)),
            pl.BlockSpec((3, tm, HEAD), (lambda j, i: (0, i % blocks_per_seq, 0)) if rope else (lambda j, i: (0, 0, 0))),
        ],
        out_specs=pl.BlockSpec((tm, tn), lambda j, i: (i, j)),
        out_shape=jax.ShapeDtypeStruct((T, K), BF16),
        scratch_shapes=[pltpu.VMEM((K, tn), BF16)],
        compiler_params=_cparams("parallel", "arbitrary"),
        name="qkv_proj",
    )(x, w, tables)


def _attn_bias():
    i = np.arange(ATTN_BQ)[:, None]
    j = np.arange(ATTN_KW)[None, :]
    band = lambda off: np.abs(j - i - off) <= ATTN_RADIUS
    r = ATTN_RADIUS
    masks = [band(0), band(r), band(r) & (j >= r), band(r) & (j < r + ATTN_BQ), band(2 * r)]
    return np.where(np.stack(masks), 0.0, NEG_INF).astype(np.float32)


def _attn_group(q_ref, k_ref, v_ref, bias_ref, o_ref, state_in, state_out, *, d, d_next):
    S = q_ref.shape[0]
    L = S // d
    assert L >= ATTN_KW and L % ATTN_BQ == 0
    log_l = int(math.log2(L))

    def next_order_rows(p0):
        if d == d_next:
            return pl.ds(p0, ATTN_BQ)
        r, u0 = p0 >> log_l, p0 & (L - 1)
        step = d // d_next
        start = (r & (d_next - 1)) * (S // d_next) + step * u0 + r // d_next
        return pl.ds(start, ATTN_BQ, stride=step)

    def one_block(blk):
        p0 = pl.multiple_of(blk * ATTN_BQ, ATTN_BQ)
        k0 = pl.multiple_of(jnp.clip(p0 - ATTN_RADIUS, 0, S - ATTN_KW), ATTN_RADIUS)
        at_start = ((p0 & (L - 1)) == 0).astype(jnp.int32)
        at_end = (((p0 + ATTN_BQ) & (L - 1)) == 0).astype(jnp.int32)
        geom = jnp.where(p0 == 0, 0, jnp.where(p0 == S - ATTN_BQ, 4, 1 + at_start + 2 * at_end))
        s = lax.dot_general(q_ref[pl.ds(p0, ATTN_BQ), :], k_ref[pl.ds(k0, ATTN_KW), :], NT_DIMS,
                            preferred_element_type=F32)
        s = s + bias_ref[geom]
        m = jnp.max(s, axis=-1, keepdims=True)
        p = jnp.exp2(s - m)
        l = jnp.sum(p, axis=-1, keepdims=True)
        acc = jnp.dot(p.astype(BF16), v_ref[pl.ds(k0, ATTN_KW), :], preferred_element_type=F32)
        m = jnp.broadcast_to(m, acc.shape)
        l = jnp.broadcast_to(l, acc.shape)
        if state_in is not None:
            m_ref, l_ref, acc_ref = state_in
            own = pl.ds(p0, ATTN_BQ)
            m_old = m_ref[own, :]
            m_new = jnp.maximum(m_old, m)
            a_old = jnp.exp2(m_old - m_new)
            a_new = jnp.exp2(m - m_new)
            l = a_old * l_ref[own, :] + a_new * l
            acc = a_old * acc_ref[own, :] + a_new * acc
            m = m_new
        dst = next_order_rows(p0)
        if state_out is None:
            o_ref[dst, :] = (acc / l).astype(o_ref.dtype)
        else:
            m_ref, l_ref, acc_ref = state_out
            m_ref[dst, :] = m
            l_ref[dst, :] = l
            acc_ref[dst, :] = acc

    unroll = ATTN_UNROLL

    def block_body(i, carry):
        for u in range(unroll):
            one_block(i * unroll + u)
        return carry

    lax.fori_loop(0, S // (ATTN_BQ * unroll), block_body, 0)


def _attn_kernel(*refs):
    n_groups = len(DILATED_GROUPS)
    qkv_refs = refs[:3 * n_groups]
    bias_ref, o_ref = refs[3 * n_groups:3 * n_groups + 2]
    states = [refs[3 * n_groups + 2:][3 * i:3 * i + 3] for i in range(2)]
    assert DILATED_GROUPS[0][1] == 1
    order = list(reversed(range(n_groups)))
    for step, gi in enumerate(order):
        window, dilation = DILATED_GROUPS[gi]
        assert window // (2 * dilation) == ATTN_RADIUS
        last = step == n_groups - 1
        q_ref, k_ref, v_ref = qkv_refs[3 * gi:3 * gi + 3]
        _attn_group(q_ref, k_ref, v_ref, bias_ref, o_ref,
                    None if step == 0 else states[(step - 1) % 2],
                    None if last else states[step % 2],
                    d=dilation, d_next=1 if last else DILATED_GROUPS[order[step + 1]][1])


def _attn_call(qkv, B, S, D):
    H = D // HEAD
    head = pl.BlockSpec((S, HEAD), lambda b, h: (b, h))
    return pl.pallas_call(
        _attn_kernel,
        grid=(B, H),
        in_specs=[head] * len(qkv) + [pl.BlockSpec((5, ATTN_BQ, ATTN_KW), lambda b, h: (0, 0, 0))],
        out_specs=head,
        out_shape=jax.ShapeDtypeStruct((B * S, D), BF16),
        scratch_shapes=[pltpu.VMEM((S, HEAD), F32)] * 6,
        compiler_params=_cparams("parallel", "parallel"),
        name="dilated_attention",
    )(*qkv, jnp.asarray(_attn_bias()))


def kernel(x, norm_w, w_in_hgrn, hgrn_lower_bounds, hgrn_gnorm, w_in_attn, w_out, w_ffn_in, w_ffn_out):
    B, S, D = x.shape
    depth = norm_w.shape[0]
    T = B * S
    xf = x.reshape(T, D).astype(F32)
    lb = _lb_call(hgrn_lower_bounds).reshape(2 * depth, 1, D)
    tables = _rope_tables(S)

    hn = _norm_call(xf, w_pre=norm_w[0, 0])
    for layer in range(depth):
        slot = layer // 2
        if layer % 2 == 0:
            proj = _matmul(hn, w_in_hgrn, slot, tm=IN_PROJ_TILE[0], tn=IN_PROJ_TILE[1], out_dtype=F32)
            mixed = _hgrn_call(proj, lb, layer, depth, hgrn_gnorm[slot], B, S, D)
        else:
            qkv = []
            for gi in range(len(DILATED_GROUPS)):
                for kind in range(3):
                    tabs = None if kind == 2 else tables[gi, 3 * kind:3 * kind + 3]
                    qkv.append(_qkv_call(hn_by_group[gi], w_in_attn, slot, (gi * 3 + kind) * D, tabs, S,
                                         tm=IN_PROJ_TILE[0], tn=IN_PROJ_TILE[1]))
            mixed = _attn_call(qkv, B, S, D)
        xf, hn = _outproj_merge(mixed, w_out, layer, xf, norm_w[layer, 1], norm_w[layer, 2], tm=OUT_PROJ_ROWS)
        act = _matmul_swiglu(hn, w_ffn_in, layer, tm=SWIGLU_TILE[0], tn=SWIGLU_TILE[1])
        h = _matmul(act, w_ffn_out, layer, tm=FFN_OUT_TILE[0], tn=FFN_OUT_TILE[1], out_dtype=BF16)
        if layer + 1 < depth:
            dil = tuple(d for _, d in DILATED_GROUPS if d > 1) if (layer + 1) % 2 == 1 else ()
            xf, hn, *hn_perm = _norm_call(xf, h, norm_w[layer, 3], norm_w[layer + 1, 0], dilations=dil, seq=S)
            hn_by_group = [hn] + hn_perm
        else:
            xf = _norm_call(xf, h, norm_w[layer, 3])
    return xf.reshape(B, S, D).astype(x.dtype)
```
